```python
import jax, jax.numpy as jnp
from jax import lax
import numpy as np

D_MODEL = 1024
BATCH = 16
SEQ = 2048
DEPTH = 4

D_CONV = D_MODEL
CONV_A_WIDTH = 31
D_RNN = 1536
N_RNN_HEADS = 16
RNN_HEAD_DIM = D_RNN // N_RNN_HEADS
CONV_B_WIDTH = 4
LRU_C = 8.0
D_FF = 4 * D_MODEL
EPS = 1e-6

SPLITS = (D_CONV, D_CONV, D_RNN, D_RNN, D_MODEL, D_MODEL)
D_IN = sum(SPLITS)

kernel_name = "hybrid_conformer_conv_rglru_gated_parallel"


def rms_norm(x, g):
    xf = x.astype(jnp.float32)
    y = xf * lax.rsqrt(jnp.mean(xf * xf, axis=-1, keepdims=True) + EPS)
    return (y * g.astype(jnp.float32)).astype(x.dtype)


def layer_norm(x, g, b):
    xf = x.astype(jnp.float32)
    mu = jnp.mean(xf, axis=-1, keepdims=True)
    xc = xf - mu
    y = xc * lax.rsqrt(jnp.mean(xc * xc, axis=-1, keepdims=True) + EPS)
    return (y * g.astype(jnp.float32) + b.astype(jnp.float32)).astype(x.dtype)


def causal_depthwise_conv(u, w, b):
    k = w.shape[0]
    y = lax.conv_general_dilated(
        u, w[:, None, :].astype(u.dtype), window_strides=(1,), padding=[(k - 1, 0)],
        dimension_numbers=("NWC", "WIO", "NWC"), feature_group_count=u.shape[-1])
    return y + b


def block_diag_linear(x, w, b):
    bsz, s, c = x.shape
    xh = x.reshape(bsz, s, N_RNN_HEADS, RNN_HEAD_DIM)
    y = jnp.einsum("bshi,hij->bshj", xh, w).reshape(bsz, s, c)
    return y + b


def rg_lru(x, w_a, b_a, w_x, b_x, lam):
    s = x.shape[1]
    r = jax.nn.sigmoid(block_diag_linear(x, w_a, b_a).astype(jnp.float32))
    i = jax.nn.sigmoid(block_diag_linear(x, w_x, b_x).astype(jnp.float32))
    log_a = -LRU_C * r * jax.nn.softplus(-lam.astype(jnp.float32))
    a = jnp.exp(log_a)
    mult = jnp.sqrt(-jnp.expm1(2.0 * log_a))
    is_start = (jnp.arange(s) == 0)[None, :, None]
    mult = jnp.where(is_start, 1.0, mult)
    bterm = mult * i * x.astype(jnp.float32)

    def combine(left, right):
        a_l, b_l = left
        a_r, b_r = right
        return a_r * a_l, a_r * b_l + b_r

    _, h = lax.associative_scan(combine, (a, bterm), axis=1)
    return h.astype(x.dtype)


def setup_inputs(seed: int = 0) -> dict:
    key = jax.random.key(seed)
    ks = jax.random.split(key, 32)
    f32 = jnp.float32
    L = DEPTH

    def nrm(k, shape, scale):
        return jax.random.normal(k, shape, f32) * scale

    x = jax.random.normal(ks[0], (BATCH, SEQ, D_MODEL), f32)
    g_mix = 1.0 + nrm(ks[1], (L, D_MODEL), 0.02)
    w_in = nrm(ks[2], (L, D_MODEL, D_IN), D_MODEL ** -0.5)
    b_in = nrm(ks[3], (L, D_IN), 0.02)
    conv_a_w = nrm(ks[4], (L, CONV_A_WIDTH, D_CONV), CONV_A_WIDTH ** -0.5)
    conv_a_b = nrm(ks[5], (L, D_CONV), 0.02)
    ln_g = 1.0 + nrm(ks[6], (L, D_CONV), 0.02)
    ln_b = nrm(ks[7], (L, D_CONV), 0.02)
    w_a_out = nrm(ks[8], (L, D_CONV, D_MODEL), D_CONV ** -0.5)
    conv_b_w = nrm(ks[9], (L, CONV_B_WIDTH, D_RNN), CONV_B_WIDTH ** -0.5)
    conv_b_b = nrm(ks[10], (L, D_RNN), 0.02)
    w_rg_a = nrm(ks[11], (L, N_RNN_HEADS, RNN_HEAD_DIM, RNN_HEAD_DIM), RNN_HEAD_DIM ** -0.5)
    b_rg_a = nrm(ks[12], (L, D_RNN), 0.02)
    w_rg_x = nrm(ks[13], (L, N_RNN_HEADS, RNN_HEAD_DIM, RNN_HEAD_DIM), RNN_HEAD_DIM ** -0.5)
    b_rg_x = nrm(ks[14], (L, D_RNN), 0.02)
    a0 = jax.random.uniform(ks[15], (L, D_RNN), f32, 0.9, 0.999)
    s0 = a0 ** (1.0 / LRU_C)
    lam = jnp.log(s0) - jnp.log1p(-s0)
    w_b_out = nrm(ks[16], (L, D_RNN, D_MODEL), D_RNN ** -0.5)
    w_o = nrm(ks[17], (L, D_MODEL, D_MODEL), D_MODEL ** -0.5)
    g_mlp = 1.0 + nrm(ks[18], (L, D_MODEL), 0.02)
    w_1 = nrm(ks[19], (L, D_MODEL, D_FF), D_MODEL ** -0.5)
    w_2 = nrm(ks[20], (L, D_FF, D_MODEL), D_FF ** -0.5)
    g_final = 1.0 + nrm(ks[21], (D_MODEL,), 0.02)
    return {"x": x, "g_mix": g_mix, "w_in": w_in, "b_in": b_in,
            "conv_a_w": conv_a_w, "conv_a_b": conv_a_b, "ln_g": ln_g, "ln_b": ln_b,
            "w_a_out": w_a_out, "conv_b_w": conv_b_w, "conv_b_b": conv_b_b,
            "w_rg_a": w_rg_a, "b_rg_a": b_rg_a, "w_rg_x": w_rg_x, "b_rg_x": b_rg_x,
            "lam": lam, "w_b_out": w_b_out, "w_o": w_o, "g_mlp": g_mlp,
            "w_1": w_1, "w_2": w_2, "g_final": g_final}


def reference(x, g_mix, w_in, b_in, conv_a_w, conv_a_b, ln_g, ln_b, w_a_out,
              conv_b_w, conv_b_b, w_rg_a, b_rg_a, w_rg_x, b_rg_x, lam, w_b_out,
              w_o, g_mlp, w_1, w_2, g_final):
    cuts = np.cumsum(SPLITS)[:-1].tolist()
    for l in range(DEPTH):
        h = rms_norm(x, g_mix[l])
        z = jnp.einsum("bsd,de->bse", h, w_in[l]) + b_in[l]
        va, ga, xb, gb, sa, sb = jnp.split(z, cuts, axis=-1)

        u = va * jax.nn.sigmoid(ga)
        u = causal_depthwise_conv(u, conv_a_w[l], conv_a_b[l])
        u = jax.nn.silu(layer_norm(u, ln_g[l], ln_b[l]))
        y_a = jnp.einsum("bsc,cd->bsd", u, w_a_out[l])

        v = causal_depthwise_conv(xb, conv_b_w[l], conv_b_b[l])
        v = rg_lru(v, w_rg_a[l], b_rg_a[l], w_rg_x[l], b_rg_x[l], lam[l])
        y_b = jnp.einsum("bsc,cd->bsd", v * jax.nn.gelu(gb), w_b_out[l])

        m = jax.nn.sigmoid(sa) * y_a + jax.nn.sigmoid(sb) * y_b
        x = x + jnp.einsum("bsd,de->bse", m, w_o[l])

        h = rms_norm(x, g_mlp[l])
        f = jnp.square(jax.nn.relu(jnp.einsum("bsd,df->bsf", h, w_1[l])))
        x = x + jnp.einsum("bsf,fd->bsd", f, w_2[l])
    return rms_norm(x, g_final)
```

```python
import functools

import jax
import jax.numpy as jnp
from jax import lax
from jax.experimental import pallas as pl
from jax.experimental.pallas import tpu as pltpu

LRU_C = 8.0
EPS = 1e-6

V7X_LANES = 128
V7X_MXU_DIM = 256
V7X_VMEM_BYTES = 64 * 1024 * 1024

MIXER_STEPS = 16
MLP_ROWS = 512
COL_BLOCK = 512
CONV_ROWS = 64
FF_BLOCK = 1024

_BF16 = jnp.bfloat16
_F32 = jnp.float32


def _dot(a, b):
    return jnp.dot(a, b, preferred_element_type=_F32)


def _rms(x, g):
    return x * lax.rsqrt(jnp.mean(x * x, axis=-1, keepdims=True) + EPS) * g


def _band_ranges(n_heads, head_dim):
    d = n_heads * head_dim
    out = []
    for c0 in range(0, d, V7X_MXU_DIM):
        c1 = c0 + V7X_MXU_DIM
        k0 = (c0 // head_dim) * head_dim
        k1 = ((c1 - 1) // head_dim + 1) * head_dim
        k0 = k0 // V7X_LANES * V7X_LANES
        k1 = min(-(-k1 // V7X_LANES) * V7X_LANES, d)
        out.append((k0, k1))
    return tuple(out)


def _block_diag(w):
    n, hd, _ = w.shape
    eye = jnp.eye(n, dtype=w.dtype)
    return (w[:, :, None, :] * eye[:, None, :, None]).reshape(n * hd, n * hd)


def _pack_band(w_a, w_x, ranges):
    da, dx = _block_diag(w_a), _block_diag(w_x)
    parts = []
    for j, (k0, k1) in enumerate(ranges):
        c0, c1 = j * V7X_MXU_DIM, (j + 1) * V7X_MXU_DIM
        parts.append(jnp.concatenate([da[k0:k1, c0:c1], dx[k0:k1, c0:c1]], axis=1))
    return jnp.concatenate(parts, axis=0)


def _mixer_kernel(x_ref, g_ref, win_ref, bin_ref, caw_ref, cab_ref, lng_ref, lnb_ref, waout_ref,
                  cbw_ref, cbb_ref, wband_ref, brg_ref, lam_ref, wbout_ref, wo_ref,
                  out_ref,
                  hbuf, ubuf, cbuf, xbuf, vbuf, abuf, bbuf, mbuf, hstate,
                  *, batch, ranges):
    rows, d_model = x_ref.shape
    d_conv = caw_ref.shape[1]
    d_rnn = cbw_ref.shape[1]
    ka = caw_ref.shape[0]
    kb = cbw_ref.shape[0]
    halo_a = (ka - 1) * batch
    halo_b = (kb - 1) * batch
    o_va, o_ga = 0, d_conv
    o_xb = 2 * d_conv
    o_gb = o_xb + d_rnn
    o_sa = o_gb + d_rnn
    o_sb = o_sa + d_model
    pid = pl.program_id(0)

    @pl.when(pid == 0)
    def _():
        ubuf[0:halo_a, :] = jnp.zeros((halo_a, d_conv), _F32)
        xbuf[0:halo_b, :] = jnp.zeros((halo_b, d_rnn), _F32)
        hstate[...] = jnp.zeros(hstate.shape, _F32)

    def proj(off, c, n):
        return _dot(hbuf[...], win_ref[:, off + c:off + c + n]) + bin_ref[:, off + c:off + c + n]

    hbuf[...] = _rms(x_ref[...], g_ref[...]).astype(_BF16)

    for c in range(0, d_conv, COL_BLOCK):
        va = proj(o_va, c, COL_BLOCK)
        ga = proj(o_ga, c, COL_BLOCK)
        ubuf[halo_a:halo_a + rows, c:c + COL_BLOCK] = va * jax.nn.sigmoid(ga)

    def conv_body(i, carry):
        r0 = pl.multiple_of(i * CONV_ROWS, CONV_ROWS)
        for c in range(0, d_conv, V7X_LANES):
            acc = jnp.broadcast_to(cab_ref[:, c:c + V7X_LANES], (CONV_ROWS, V7X_LANES))
            for k in range(ka):
                acc = acc + caw_ref[k:k + 1, c:c + V7X_LANES] * ubuf[pl.ds(r0 + k * batch, CONV_ROWS), c:c + V7X_LANES]
            cbuf[pl.ds(r0, CONV_ROWS), c:c + V7X_LANES] = acc
        return carry

    lax.fori_loop(0, rows // CONV_ROWS, conv_body, 0)

    cv = cbuf[...]
    mu = jnp.mean(cv, axis=-1, keepdims=True)
    xc = cv - mu
    yn = xc * lax.rsqrt(jnp.mean(xc * xc, axis=-1, keepdims=True) + EPS) * lng_ref[...] + lnb_ref[...]
    act = (yn * jax.nn.sigmoid(yn)).astype(_BF16)
    y_a = _dot(act, waout_ref[...])
    for c in range(0, d_model, COL_BLOCK):
        sa = proj(o_sa, c, COL_BLOCK)
        mbuf[:, c:c + COL_BLOCK] = jax.nn.sigmoid(sa) * y_a[:, c:c + COL_BLOCK]

    for c in range(0, d_rnn, COL_BLOCK):
        xbuf[halo_b:halo_b + rows, c:c + COL_BLOCK] = proj(o_xb, c, COL_BLOCK)
    v = jnp.broadcast_to(cbb_ref[...], (rows, d_rnn))
    for k in range(kb):
        v = v + cbw_ref[k:k + 1, :] * xbuf[k * batch:k * batch + rows, :]
    bbuf[...] = v
    vbuf[...] = v.astype(_BF16)

    x_lam = -lam_ref[...]
    softplus_neg_lam = jnp.maximum(x_lam, 0.0) + jnp.log1p(jnp.exp(-jnp.abs(x_lam)))
    row = pid * rows + lax.broadcasted_iota(jnp.int32, (rows, 1), 0)
    is_start = row < batch
    off = 0
    for j, (k0, k1) in enumerate(ranges):
        c0, c1 = j * V7X_MXU_DIM, (j + 1) * V7X_MXU_DIM
        gates = _dot(vbuf[:, k0:k1], wband_ref[off:off + (k1 - k0), :])
        off += k1 - k0
        r_gate = jax.nn.sigmoid(gates[:, :V7X_MXU_DIM] + brg_ref[0:1, c0:c1])
        i_gate = jax.nn.sigmoid(gates[:, V7X_MXU_DIM:] + brg_ref[1:2, c0:c1])
        log_a = (-LRU_C) * r_gate * softplus_neg_lam[:, c0:c1]
        a = jnp.exp(log_a)
        mult = jnp.sqrt(jnp.tanh(-log_a) * (1.0 + a * a))
        mult = jnp.where(is_start, 1.0, mult)
        abuf[:, c0:c1] = a
        bbuf[:, c0:c1] = mult * i_gate * bbuf[:, c0:c1]

    def scan_body(t, h):
        r0 = pl.multiple_of(t * batch, batch)
        h = abuf[pl.ds(r0, batch), :] * h + bbuf[pl.ds(r0, batch), :]
        bbuf[pl.ds(r0, batch), :] = h
        return h

    hstate[...] = lax.fori_loop(0, rows // batch, scan_body, hstate[...])

    for c in range(0, d_rnn, COL_BLOCK):
        gb = proj(o_gb, c, COL_BLOCK)
        vbuf[:, c:c + COL_BLOCK] = (bbuf[:, c:c + COL_BLOCK] * jax.nn.gelu(gb)).astype(_BF16)
    y_b = _dot(vbuf[...], wbout_ref[...])

    for c in range(0, d_model, COL_BLOCK):
        sb = proj(o_sb, c, COL_BLOCK)
        mbuf[:, c:c + COL_BLOCK] = mbuf[:, c:c + COL_BLOCK] + jax.nn.sigmoid(sb) * y_b[:, c:c + COL_BLOCK]
    out_ref[...] = x_ref[...] + _dot(mbuf[...].astype(_BF16), wo_ref[...])

    for s in range(0, halo_a, rows):
        n = min(rows, halo_a - s)
        ubuf[s:s + n, :] = ubuf[rows + s:rows + s + n, :]
    for s in range(0, halo_b, rows):
        n = min(rows, halo_b - s)
        xbuf[s:s + n, :] = xbuf[rows + s:rows + s + n, :]


def _mlp_kernel(x_ref, g_ref, w1_ref, w2_ref, gfin_ref, out_ref, hbuf, *, final):
    d_ff = w1_ref.shape[1]
    x = x_ref[...]
    hbuf[...] = _rms(x, g_ref[...]).astype(_BF16)
    acc = x
    for c in range(0, d_ff, FF_BLOCK):
        f = _dot(hbuf[...], w1_ref[:, c:c + FF_BLOCK])
        f = jnp.square(jnp.maximum(f, 0.0)).astype(_BF16)
        acc = acc + _dot(f, w2_ref[c:c + FF_BLOCK, :])
    if final:
        acc = _rms(acc, gfin_ref[...])
    out_ref[...] = acc


def _resident(shape):
    return pl.BlockSpec(shape, lambda i: (0,) * len(shape), pipeline_mode=pl.Buffered(1))


def _vmem_limit(resident_bytes, stream_bytes, scratch_bytes):
    want = resident_bytes + 2 * stream_bytes + scratch_bytes
    return min(int(want * 1.25) + (8 << 20), V7X_VMEM_BYTES - (6 << 20))


def _nbytes(*arrays):
    return sum(a.size * a.dtype.itemsize for a in arrays)


def _mixer(x, g, w_in, b_in, caw, cab, lng, lnb, w_a_out, cbw, cbb, wband, brg, lam, w_b_out, w_o, *, batch, ranges):
    n_rows, d_model = x.shape
    rows = MIXER_STEPS * batch
    d_conv, d_rnn = caw.shape[1], cbw.shape[1]
    halo_a, halo_b = (caw.shape[0] - 1) * batch, (cbw.shape[0] - 1) * batch
    weights = (g, w_in, b_in, caw, cab, lng, lnb, w_a_out, cbw, cbb, wband, brg, lam, w_b_out, w_o)
    scratch = [
        pltpu.VMEM((rows, d_model), _BF16),
        pltpu.VMEM((halo_a + rows, d_conv), _F32),
        pltpu.VMEM((rows, d_conv), _F32),
        pltpu.VMEM((halo_b + rows, d_rnn), _F32),
        pltpu.VMEM((rows, d_rnn), _BF16),
        pltpu.VMEM((rows, d_rnn), _F32),
        pltpu.VMEM((rows, d_rnn), _F32),
        pltpu.VMEM((rows, d_model), _F32),
        pltpu.VMEM((batch, d_rnn), _F32),
    ]
    scratch_bytes = (rows * d_model * 2 + (halo_a + rows) * d_conv * 4 + rows * d_conv * 4
                     + (halo_b + rows) * d_rnn * 4 + rows * d_rnn * 2 + 2 * rows * d_rnn * 4
                     + rows * d_model * 4 + batch * d_rnn * 4)
    limit = _vmem_limit(_nbytes(*weights), 2 * rows * d_model * 4, scratch_bytes)
    row_spec = pl.BlockSpec((rows, d_model), lambda i: (i, 0))
    return pl.pallas_call(
        functools.partial(_mixer_kernel, batch=batch, ranges=ranges),
        grid=(n_rows // rows,),
        in_specs=[row_spec] + [_resident(w.shape) for w in weights],
        out_specs=row_spec,
        out_shape=jax.ShapeDtypeStruct(x.shape, x.dtype),
        scratch_shapes=scratch,
        compiler_params=pltpu.CompilerParams(dimension_semantics=("arbitrary",), vmem_limit_bytes=limit),
        name="mixer",
    )(x, *weights)


def _mlp(x, g, w1, w2, gfin, *, final):
    n_rows, d_model = x.shape
    rows = MLP_ROWS
    weights = (g, w1, w2, gfin)
    limit = _vmem_limit(_nbytes(*weights), 2 * rows * d_model * 4, rows * d_model * 2 + 3 * rows * FF_BLOCK * 4)
    row_spec = pl.BlockSpec((rows, d_model), lambda i: (i, 0))
    return pl.pallas_call(
        functools.partial(_mlp_kernel, final=final),
        grid=(n_rows // rows,),
        in_specs=[row_spec] + [_resident(w.shape) for w in weights],
        out_specs=row_spec,
        out_shape=jax.ShapeDtypeStruct(x.shape, x.dtype),
        scratch_shapes=[pltpu.VMEM((rows, d_model), _BF16)],
        compiler_params=pltpu.CompilerParams(dimension_semantics=("arbitrary",), vmem_limit_bytes=limit),
        name="mlp",
    )(x, *weights)


def kernel(x, g_mix, w_in, b_in, conv_a_w, conv_a_b, ln_g, ln_b, w_a_out, conv_b_w, conv_b_b, w_rg_a, b_rg_a,
           w_rg_x, b_rg_x, lam, w_b_out, w_o, g_mlp, w_1, w_2, g_final):
    batch, seq, d_model = x.shape
    depth = w_in.shape[0]
    n_heads, head_dim = w_rg_a.shape[1], w_rg_a.shape[2]
    assert seq % MIXER_STEPS == 0 and (batch * seq) % MLP_ROWS == 0
    assert (n_heads * head_dim) % V7X_MXU_DIM == 0 and batch % 16 == 0
    ranges = _band_ranges(n_heads, head_dim)

    def row(a):
        return a.reshape(1, -1)

    xt = jnp.transpose(x, (1, 0, 2)).reshape(seq * batch, d_model)
    for l in range(depth):
        wband = _pack_band(w_rg_a[l], w_rg_x[l], ranges).astype(_BF16)
        brg = jnp.stack([b_rg_a[l], b_rg_x[l]])
        xt = _mixer(xt, row(g_mix[l]), w_in[l].astype(_BF16), row(b_in[l]), conv_a_w[l], row(conv_a_b[l]),
                    row(ln_g[l]), row(ln_b[l]), w_a_out[l].astype(_BF16), conv_b_w[l], row(conv_b_b[l]),
                    wband, brg, row(lam[l]), w_b_out[l].astype(_BF16), w_o[l].astype(_BF16),
                    batch=batch, ranges=ranges)
        xt = _mlp(xt, row(g_mlp[l]), w_1[l].astype(_BF16), w_2[l].astype(_BF16), row(g_final),
                  final=(l == depth - 1))
    return jnp.transpose(xt.reshape(seq, batch, d_model), (1, 0, 2))
```

```python
import functools

import jax
import jax.numpy as jnp
from jax import lax
from jax.experimental import pallas as pl
from jax.experimental.pallas import tpu as pltpu

LRU_C = 8.0
EPS = 1e-6

V7X_LANES = 128
V7X_MXU_DIM = 256
V7X_VMEM_BYTES = 64 * 1024 * 1024

MIXER_STEPS = 16
MLP_ROWS = 512
COL_BLOCK = 512
FF_BLOCK = 1024

_BF16 = jnp.bfloat16
_F32 = jnp.float32


def _dot(a, b):
    return lax.dot_general(a, b, (((1,), (0,)), ((), ())), preferred_element_type=_F32)


def _rms(x, g):
    return x * lax.rsqrt(jnp.mean(x * x, axis=-1, keepdims=True) + EPS) * g


def _sigmoid(x):
    return 0.5 * jnp.tanh(0.5 * x) + 0.5


def _band_ranges(n_heads, head_dim):
    d = n_heads * head_dim
    out = []
    for c0 in range(0, d, V7X_MXU_DIM):
        c1 = c0 + V7X_MXU_DIM
        k0 = (c0 // head_dim) * head_dim
        k1 = ((c1 - 1) // head_dim + 1) * head_dim
        k0 = k0 // V7X_LANES * V7X_LANES
        k1 = min(-(-k1 // V7X_LANES) * V7X_LANES, d)
        out.append((k0, k1))
    return tuple(out)


def _block_diag(w):
    n, hd, _ = w.shape
    eye = jnp.eye(n, dtype=w.dtype)
    return (w[:, :, None, :] * eye[:, None, :, None]).reshape(n * hd, n * hd)


def _pack_band(w_a, w_x, ranges):
    da, dx = _block_diag(w_a), _block_diag(w_x)
    parts = []
    for j, (k0, k1) in enumerate(ranges):
        c0, c1 = j * V7X_MXU_DIM, (j + 1) * V7X_MXU_DIM
        parts.append(jnp.concatenate([da[k0:k1, c0:c1], dx[k0:k1, c0:c1]], axis=1))
    return jnp.concatenate(parts, axis=0)


def _col_blocks(w, n):
    k = w.shape[0]
    return jnp.transpose(w.reshape(k, -1, n), (1, 0, 2))


def _glu_blocks(a, b):
    k = a.shape[0]
    both = jnp.concatenate([a.reshape(k, -1, V7X_LANES), b.reshape(k, -1, V7X_LANES)], axis=2)
    return jnp.transpose(both, (1, 0, 2))


def _mixer_kernel(x_ref, g_ref, wglu_ref, bglu_ref, wside_ref, bside_ref, caw_ref, cab_ref, lng_ref, lnb_ref,
                  waout_ref, cbw_ref, cbb_ref, wband_ref, brg_ref, lam_ref, wbout_ref, wo_ref,
                  out_ref,
                  hbuf, ubuf, cbuf, zbuf, actbuf, vbuf, abuf, bbuf, mbuf, hstate,
                  *, batch, ranges):
    rows, d_model = x_ref.shape
    n_groups, ka, _ = caw_ref.shape
    kb, d_rnn = cbw_ref.shape
    halo_a = (ka - 1) * batch
    halo_b = (kb - 1) * batch
    n_side = wside_ref.shape[0]
    side_per_step = 2
    n_xb = d_rnn // V7X_MXU_DIM
    s_gb, s_sa, s_sb = n_xb, 2 * n_xb, 2 * n_xb + d_model // V7X_MXU_DIM
    pid = pl.program_id(0)

    @pl.when(pid == 0)
    def _():
        ubuf[:, 0:halo_a, :] = jnp.zeros((n_groups, halo_a, V7X_LANES), _F32)
        zbuf[0:n_xb, 0:halo_b, :] = jnp.zeros((n_xb, halo_b, V7X_MXU_DIM), _F32)
        hstate[...] = jnp.zeros(hstate.shape, _F32)

    def glu_block(g):
        z = _dot(hbuf[...], wglu_ref[g]) + bglu_ref[g]
        ubuf[g, halo_a:halo_a + rows, :] = z[:, :V7X_LANES] * _sigmoid(z[:, V7X_LANES:])

    def side_block(s):
        zbuf[s, halo_b:halo_b + rows, :] = _dot(hbuf[...], wside_ref[s]) + bside_ref[s]

    def conv_group(g):
        acc = jnp.broadcast_to(cab_ref[g], (rows, V7X_LANES))
        for k in range(ka):
            acc = acc + caw_ref[g, k:k + 1, :] * ubuf[g, k * batch:k * batch + rows, :]
        cbuf[g] = acc

    def side(s):
        return zbuf[s, halo_b:halo_b + rows, :]

    hbuf[...] = _rms(x_ref[...], g_ref[...]).astype(_BF16)

    def loop_body(g, carry):
        glu_block(g)
        conv_group(g)
        for s in range(side_per_step):
            side_block(side_per_step * g + s)
        return carry

    lax.fori_loop(0, n_groups, loop_body, 0)

    for s in range(side_per_step * n_groups, n_side):
        side_block(s)
    for i in range(n_xb):
        c = i * V7X_MXU_DIM
        v = jnp.broadcast_to(cbb_ref[:, c:c + V7X_MXU_DIM], (rows, V7X_MXU_DIM))
        for k in range(kb):
            v = v + cbw_ref[k:k + 1, c:c + V7X_MXU_DIM] * zbuf[i, k * batch:k * batch + rows, :]
        vbuf[:, c:c + V7X_MXU_DIM] = v
    cv = jnp.concatenate([cbuf[g] for g in range(n_groups)], axis=1)
    mu = jnp.mean(cv, axis=-1, keepdims=True)
    xc = cv - mu
    yn = xc * lax.rsqrt(jnp.mean(xc * xc, axis=-1, keepdims=True) + EPS) * lng_ref[...] + lnb_ref[...]
    actbuf[...] = yn * _sigmoid(yn)

    x_lam = -lam_ref[...]
    softplus_neg_lam = jnp.maximum(x_lam, 0.0) + jnp.log1p(jnp.exp(-jnp.abs(x_lam)))
    half_decay = (-0.5 * LRU_C) * softplus_neg_lam
    row = pid * rows + lax.broadcasted_iota(jnp.int32, (rows, 1), 0)
    is_start = row < batch
    off = 0
    for j, (k0, k1) in enumerate(ranges):
        c0, c1 = j * V7X_MXU_DIM, (j + 1) * V7X_MXU_DIM
        gates = _dot(vbuf[:, k0:k1], wband_ref[off:off + (k1 - k0), :])
        off += k1 - k0
        i_gate = _sigmoid(gates[:, V7X_MXU_DIM:] + brg_ref[1:2, c0:c1])
        tanh_r = jnp.tanh(0.5 * (gates[:, :V7X_MXU_DIM] + brg_ref[0:1, c0:c1]))
        log_a = (tanh_r + 1.0) * half_decay[:, c0:c1]
        a = jnp.exp(log_a)
        s = jnp.tanh(-log_a) * (1.0 + a * a)
        mult = jnp.where(s > 0.0, s * lax.rsqrt(s), 0.0)
        mult = jnp.where(is_start, 1.0, mult)
        abuf[:, c0:c1] = a
        bbuf[:, c0:c1] = mult * i_gate * vbuf[:, c0:c1]
        if j < d_model // V7X_MXU_DIM:
            y_a = _dot(actbuf[...], waout_ref[j])
            mbuf[:, c0:c1] = _sigmoid(side(s_sa + j)) * y_a

    h = hstate[...]
    for t in range(rows // batch):
        r0 = t * batch
        h = abuf[r0:r0 + batch, :] * h + bbuf[r0:r0 + batch, :]
        bbuf[r0:r0 + batch, :] = h
    hstate[...] = h

    for i in range(n_xb):
        c = i * V7X_MXU_DIM
        vbuf[:, c:c + V7X_MXU_DIM] = bbuf[:, c:c + V7X_MXU_DIM] * jax.nn.gelu(side(s_gb + i))
    sub = COL_BLOCK // V7X_MXU_DIM
    for blk in range(d_model // COL_BLOCK):
        c = blk * COL_BLOCK
        y_b = _dot(vbuf[...], wbout_ref[blk])
        gate_b = jnp.concatenate([_sigmoid(side(s_sb + sub * blk + i)) for i in range(sub)], axis=1)
        mbuf[:, c:c + COL_BLOCK] = mbuf[:, c:c + COL_BLOCK] + gate_b * y_b
    for blk in range(d_model // COL_BLOCK):
        c = blk * COL_BLOCK
        out_ref[:, c:c + COL_BLOCK] = x_ref[:, c:c + COL_BLOCK] + _dot(mbuf[...], wo_ref[blk])

    for s0 in range(0, halo_a, rows):
        n = min(rows, halo_a - s0)
        ubuf[:, s0:s0 + n, :] = ubuf[:, rows + s0:rows + s0 + n, :]
    for s0 in range(0, halo_b, rows):
        n = min(rows, halo_b - s0)
        zbuf[0:n_xb, s0:s0 + n, :] = zbuf[0:n_xb, rows + s0:rows + s0 + n, :]


def _mlp_kernel(x_ref, g_ref, w1_ref, w2_ref, gfin_ref, out_ref, hbuf, *, final):
    d_ff = w1_ref.shape[1]
    x = x_ref[...]
    hbuf[...] = _rms(x, g_ref[...]).astype(_BF16)
    acc = x
    for c in range(0, d_ff, FF_BLOCK):
        f = _dot(hbuf[...], w1_ref[:, c:c + FF_BLOCK])
        f = jnp.square(jnp.maximum(f, 0.0)).astype(_BF16)
        acc = acc + _dot(f, w2_ref[c:c + FF_BLOCK, :])
    if final:
        acc = _rms(acc, gfin_ref[...])
    out_ref[...] = acc


def _resident(shape):
    return pl.BlockSpec(shape, lambda i: (0,) * len(shape), pipeline_mode=pl.Buffered(1))


def _nbytes(*arrays):
    return sum(a.size * a.dtype.itemsize for a in arrays)


def _vmem_limit(weights, row_block_bytes, scratch_bytes, temp_bytes):
    want = _nbytes(*weights) + 2 * 2 * row_block_bytes + scratch_bytes + temp_bytes
    return min(want, V7X_VMEM_BYTES)


def _mixer(x, g, wglu, bglu, wside, bside, caw, cab, lng, lnb, w_a_out, cbw, cbb, wband, brg, lam, w_b_out, w_o,
           *, batch, ranges):
    n_rows, d_model = x.shape
    rows = MIXER_STEPS * batch
    n_groups, ka, _ = caw.shape
    kb, d_rnn = cbw.shape
    halo_a, halo_b = (ka - 1) * batch, (kb - 1) * batch
    weights = (g, wglu, bglu, wside, bside, caw, cab, lng, lnb, w_a_out, cbw, cbb, wband, brg, lam, w_b_out, w_o)
    scratch_shapes = (
        (rows, d_model),
        (n_groups, halo_a + rows, V7X_LANES),
        (n_groups, rows, V7X_LANES),
        (wside.shape[0], halo_b + rows, V7X_MXU_DIM),
        (rows, n_groups * V7X_LANES),
        (rows, d_rnn),
        (rows, d_rnn),
        (rows, d_rnn),
        (rows, d_model),
        (batch, d_rnn),
    )
    scratch_bytes = sum(4 * functools.reduce(lambda a, b: a * b, s) for s in scratch_shapes)
    limit = _vmem_limit(weights, rows * d_model * 4, scratch_bytes, 6 * rows * d_rnn * 4)
    row_spec = pl.BlockSpec((rows, d_model), lambda i: (i, 0))
    return pl.pallas_call(
        functools.partial(_mixer_kernel, batch=batch, ranges=ranges),
        grid=(n_rows // rows,),
        in_specs=[row_spec] + [_resident(w.shape) for w in weights],
        out_specs=row_spec,
        out_shape=jax.ShapeDtypeStruct(x.shape, x.dtype),
        scratch_shapes=[pltpu.VMEM(s, _BF16 if i == 0 else _F32) for i, s in enumerate(scratch_shapes)],
        compiler_params=pltpu.CompilerParams(dimension_semantics=("arbitrary",), vmem_limit_bytes=limit),
        name="mixer",
    )(x, *weights)


def _mlp(x, g, w1, w2, gfin, *, final):
    n_rows, d_model = x.shape
    rows = MLP_ROWS
    weights = (g, w1, w2, gfin)
    limit = _vmem_limit(weights, rows * d_model * 4, rows * d_model * 2, 4 * rows * FF_BLOCK * 4)
    row_spec = pl.BlockSpec((rows, d_model), lambda i: (i, 0))
    return pl.pallas_call(
        functools.partial(_mlp_kernel, final=final),
        grid=(n_rows // rows,),
        in_specs=[row_spec] + [_resident(w.shape) for w in weights],
        out_specs=row_spec,
        out_shape=jax.ShapeDtypeStruct(x.shape, x.dtype),
        scratch_shapes=[pltpu.VMEM((rows, d_model), _BF16)],
        compiler_params=pltpu.CompilerParams(dimension_semantics=("arbitrary",), vmem_limit_bytes=limit),
        name="mlp",
    )(x, *weights)


def kernel(x, g_mix, w_in, b_in, conv_a_w, conv_a_b, ln_g, ln_b, w_a_out, conv_b_w, conv_b_b, w_rg_a, b_rg_a,
           w_rg_x, b_rg_x, lam, w_b_out, w_o, g_mlp, w_1, w_2, g_final):
    batch, seq, d_model = x.shape
    depth = w_in.shape[0]
    d_conv = conv_a_w.shape[2]
    n_heads, head_dim = w_rg_a.shape[1], w_rg_a.shape[2]
    assert seq % MIXER_STEPS == 0 and (batch * seq) % MLP_ROWS == 0
    assert (n_heads * head_dim) % V7X_MXU_DIM == 0 and batch % 16 == 0
    assert d_conv % V7X_MXU_DIM == 0 and d_model % COL_BLOCK == 0
    ranges = _band_ranges(n_heads, head_dim)

    def row(a):
        return a.reshape(1, -1)

    xt = jnp.transpose(x, (1, 0, 2)).reshape(seq * batch, d_model)
    for l in range(depth):
        w, b = w_in[l], b_in[l][None, :]
        wglu = _glu_blocks(w[:, :d_conv], w[:, d_conv:2 * d_conv]).astype(_BF16)
        bglu = _glu_blocks(b[:, :d_conv], b[:, d_conv:2 * d_conv])
        wside = _col_blocks(w[:, 2 * d_conv:], V7X_MXU_DIM).astype(_BF16)
        bside = _col_blocks(b[:, 2 * d_conv:], V7X_MXU_DIM)
        caw = _col_blocks(conv_a_w[l], V7X_LANES)
        cab = _col_blocks(conv_a_b[l][None, :], V7X_LANES)
        wband = _pack_band(w_rg_a[l], w_rg_x[l], ranges).astype(_BF16)
        brg = jnp.stack([b_rg_a[l], b_rg_x[l]])
        xt = _mixer(xt, row(g_mix[l]), wglu, bglu, wside, bside, caw, cab, row(ln_g[l]), row(ln_b[l]),
                    _col_blocks(w_a_out[l], V7X_MXU_DIM).astype(_BF16), conv_b_w[l], row(conv_b_b[l]),
                    wband, brg, row(lam[l]), _col_blocks(w_b_out[l], COL_BLOCK).astype(_BF16),
                    _col_blocks(w_o[l], COL_BLOCK).astype(_BF16), batch=batch, ranges=ranges)
        xt = _mlp(xt, row(g_mlp[l]), w_1[l].astype(_BF16), w_2[l].astype(_BF16), row(g_final),
                  final=(l == depth - 1))
    return jnp.transpose(xt.reshape(seq, batch, d_model), (1, 0, 2))
```

```python
import functools

import jax
import jax.numpy as jnp
from jax import lax
from jax.experimental import pallas as pl
from jax.experimental.pallas import tpu as pltpu

LRU_C = 8.0
EPS = 1e-6

V7X_LANES = 128
V7X_MXU_DIM = 256
V7X_VMEM_BYTES = 64 * 1024 * 1024

MIXER_STEPS = 16
MLP_ROWS = 512
COL_BLOCK = 512
FF_BLOCK = 1024

_BF16 = jnp.bfloat16
_F32 = jnp.float32


def _dot(a, b):
    return lax.dot_general(a, b, (((1,), (0,)), ((), ())), preferred_element_type=_F32)


def _rms(x, g):
    return x * lax.rsqrt(jnp.mean(x * x, axis=-1, keepdims=True) + EPS) * g


def _sigmoid(x):
    return 0.5 * jnp.tanh(0.5 * x) + 0.5


def _gate_groups(n_heads, head_dim):
    d = n_heads * head_dim
    out = []
    for c0 in range(0, d, V7X_LANES):
        c1 = c0 + V7X_LANES
        k0 = (c0 // head_dim) * head_dim
        k1 = ((c1 - 1) // head_dim + 1) * head_dim
        k0 = k0 // V7X_LANES * V7X_LANES
        k1 = min(-(-k1 // V7X_LANES) * V7X_LANES, d)
        out.append((k0, k1))
    return tuple(out)


def _block_diag(w):
    n, hd, _ = w.shape
    eye = jnp.eye(n, dtype=w.dtype)
    return (w[:, :, None, :] * eye[:, None, :, None]).reshape(n * hd, n * hd)


def _pack_gates(w_a, w_x, groups):
    da, dx = _block_diag(w_a), _block_diag(w_x)
    parts = []
    for q, (k0, k1) in enumerate(groups):
        c0, c1 = q * V7X_LANES, (q + 1) * V7X_LANES
        parts.append(jnp.concatenate([da[k0:k1, c0:c1], dx[k0:k1, c0:c1]], axis=1))
    return jnp.concatenate(parts, axis=0)


def _col_blocks(w, n):
    l, k, _ = w.shape
    return jnp.transpose(w.reshape(l, k, -1, n), (0, 2, 1, 3))


def _glu_blocks(a, b):
    l, k, _ = a.shape
    both = jnp.concatenate([a.reshape(l, k, -1, V7X_LANES), b.reshape(l, k, -1, V7X_LANES)], axis=3)
    return jnp.transpose(both, (0, 2, 1, 3))


def _mixer_kernel(x_ref, g_ref, wglu_ref, bglu_ref, wside_ref, bside_ref, caw_ref, cab_ref, lng_ref, lnb_ref,
                  waout_ref, cbw_ref, cbb_ref, wgate_ref, brg_ref, lam_ref, wbout_ref, wo_ref,
                  out_ref,
                  hbuf, actbuf, lhsbuf, ubuf, cbuf, zbuf, vbuf, abuf, bbuf, mbuf, hstate,
                  *, batch, groups):
    rows, d_model = x_ref.shape
    n_groups, ka, _ = caw_ref.shape
    kb, d_rnn = cbw_ref.shape
    halo_a = (ka - 1) * batch
    halo_b = (kb - 1) * batch
    n_side = wside_ref.shape[0]
    side_per_step = 2
    n_xb = d_rnn // V7X_MXU_DIM
    s_gb, s_sa, s_sb = n_xb, 2 * n_xb, 2 * n_xb + d_model // V7X_MXU_DIM
    pid = pl.program_id(0)

    @pl.when(pid == 0)
    def _():
        ubuf[:, 0:halo_a, :] = jnp.zeros((n_groups, halo_a, V7X_LANES), _F32)
        zbuf[0:n_xb, 0:halo_b, :] = jnp.zeros((n_xb, halo_b, V7X_MXU_DIM), _F32)
        hstate[...] = jnp.zeros(hstate.shape, _F32)

    def glu_block(g):
        z = _dot(hbuf[...], wglu_ref[g]) + bglu_ref[g]
        ubuf[g, halo_a:halo_a + rows, :] = z[:, :V7X_LANES] * _sigmoid(z[:, V7X_LANES:])

    def side_block(s):
        zbuf[s, halo_b:halo_b + rows, :] = _dot(hbuf[...], wside_ref[s]) + bside_ref[s]

    def conv_group(g):
        acc = jnp.broadcast_to(cab_ref[g], (rows, V7X_LANES))
        for k in range(ka):
            acc = acc + caw_ref[g, k:k + 1, :] * ubuf[g, k * batch:k * batch + rows, :]
        cbuf[g] = acc

    def side(s):
        return zbuf[s, halo_b:halo_b + rows, :]

    hbuf[...] = _rms(x_ref[...], g_ref[...]).astype(_BF16)

    def loop_body(g, carry):
        glu_block(g)
        conv_group(g)
        for s in range(side_per_step):
            side_block(side_per_step * g + s)
        return carry

    lax.fori_loop(0, n_groups, loop_body, 0)

    for s in range(side_per_step * n_groups, n_side):
        side_block(s)
    for i in range(n_xb):
        c = i * V7X_MXU_DIM
        v = jnp.broadcast_to(cbb_ref[:, c:c + V7X_MXU_DIM], (rows, V7X_MXU_DIM))
        for k in range(kb):
            v = v + cbw_ref[k:k + 1, c:c + V7X_MXU_DIM] * zbuf[i, k * batch:k * batch + rows, :]
        vbuf[:, c:c + V7X_MXU_DIM] = v
        lhsbuf[:, c:c + V7X_MXU_DIM] = v.astype(_BF16)
    cv = jnp.concatenate([cbuf[g] for g in range(n_groups)], axis=1)
    mu = jnp.mean(cv, axis=-1, keepdims=True)
    xc = cv - mu
    yn = xc * lax.rsqrt(jnp.mean(xc * xc, axis=-1, keepdims=True) + EPS) * lng_ref[...] + lnb_ref[...]
    actbuf[...] = (yn * _sigmoid(yn)).astype(_BF16)

    x_lam = -lam_ref[...]
    softplus_neg_lam = jnp.maximum(x_lam, 0.0) + jnp.log1p(jnp.exp(-jnp.abs(x_lam)))
    half_decay = (-0.5 * LRU_C) * softplus_neg_lam
    row = pid * rows + lax.broadcasted_iota(jnp.int32, (rows, 1), 0)
    is_start = row < batch
    n_ya = d_model // V7X_MXU_DIM
    off = 0
    for q, (k0, k1) in enumerate(groups):
        c0, c1 = q * V7X_LANES, (q + 1) * V7X_LANES
        gates = _dot(lhsbuf[:, k0:k1], wgate_ref[off:off + (k1 - k0), :])
        off += k1 - k0
        i_gate = _sigmoid(gates[:, V7X_LANES:] + brg_ref[1:2, c0:c1])
        tanh_r = jnp.tanh(0.5 * (gates[:, :V7X_LANES] + brg_ref[0:1, c0:c1]))
        log_a = (tanh_r + 1.0) * half_decay[:, c0:c1]
        a = jnp.exp(log_a)
        s = jnp.tanh(-log_a) * (1.0 + a * a)
        mult = jnp.where(s > 0.0, s * lax.rsqrt(s), 0.0)
        mult = jnp.where(is_start, 1.0, mult)
        abuf[:, c0:c1] = a
        bbuf[:, c0:c1] = mult * i_gate * vbuf[:, c0:c1]
        if q % (len(groups) // n_ya) == 0:
            j = q // (len(groups) // n_ya)
            y_a = _dot(actbuf[...], waout_ref[j])
            mbuf[:, j * V7X_MXU_DIM:(j + 1) * V7X_MXU_DIM] = _sigmoid(side(s_sa + j)) * y_a

    h = hstate[...]
    for t in range(rows // batch):
        r0 = t * batch
        h = abuf[r0:r0 + batch, :] * h + bbuf[r0:r0 + batch, :]
        bbuf[r0:r0 + batch, :] = h
    hstate[...] = h

    for i in range(n_xb):
        c = i * V7X_MXU_DIM
        lhsbuf[:, c:c + V7X_MXU_DIM] = (bbuf[:, c:c + V7X_MXU_DIM] * jax.nn.gelu(side(s_gb + i))).astype(_BF16)
    sub = COL_BLOCK // V7X_MXU_DIM
    for blk in range(d_model // COL_BLOCK):
        c = blk * COL_BLOCK
        y_b = _dot(lhsbuf[...], wbout_ref[blk])
        gate_b = jnp.concatenate([_sigmoid(side(s_sb + sub * blk + i)) for i in range(sub)], axis=1)
        mbuf[:, c:c + COL_BLOCK] = mbuf[:, c:c + COL_BLOCK] + gate_b * y_b
    actbuf[...] = mbuf[...].astype(_BF16)
    for blk in range(d_model // COL_BLOCK):
        c = blk * COL_BLOCK
        out_ref[:, c:c + COL_BLOCK] = x_ref[:, c:c + COL_BLOCK] + _dot(actbuf[...], wo_ref[blk])

    for s0 in range(0, halo_a, rows):
        n = min(rows, halo_a - s0)
        ubuf[:, s0:s0 + n, :] = ubuf[:, rows + s0:rows + s0 + n, :]
    for s0 in range(0, halo_b, rows):
        n = min(rows, halo_b - s0)
        zbuf[0:n_xb, s0:s0 + n, :] = zbuf[0:n_xb, rows + s0:rows + s0 + n, :]


def _mlp_kernel(x_ref, g_ref, w1_ref, w2_ref, gfin_ref, out_ref, hbuf, *, final):
    d_ff = w1_ref.shape[1]
    x = x_ref[...]
    hbuf[...] = _rms(x, g_ref[...]).astype(_BF16)
    acc = x
    for c in range(0, d_ff, FF_BLOCK):
        f = _dot(hbuf[...], w1_ref[:, c:c + FF_BLOCK])
        f = jnp.square(jnp.maximum(f, 0.0)).astype(_BF16)
        acc = acc + _dot(f, w2_ref[c:c + FF_BLOCK, :])
    if final:
        acc = _rms(acc, gfin_ref[...])
    out_ref[...] = acc


def _layer_resident(a, layer):
    zeros = (0,) * (a.ndim - 1)
    return pl.BlockSpec((None,) + a.shape[1:], lambda i: (layer,) + zeros, pipeline_mode=pl.Buffered(1))


def _layer_bytes(*arrays):
    return sum(a.size // a.shape[0] * a.dtype.itemsize for a in arrays)


def _scratch_bytes(shapes):
    return sum(functools.reduce(lambda a, b: a * b, s) * jnp.dtype(d).itemsize for s, d in shapes)


def _vmem_limit(weights, row_block_bytes, scratch_bytes, temp_bytes):
    want = _layer_bytes(*weights) + 2 * 2 * row_block_bytes + scratch_bytes + temp_bytes
    return min(want, V7X_VMEM_BYTES)


def _mixer(x, layer, weights, *, batch, groups):
    n_rows, d_model = x.shape
    rows = MIXER_STEPS * batch
    wside, caw, cbw = weights[3], weights[5], weights[10]
    _, n_groups, ka, _ = caw.shape
    assert n_groups * V7X_LANES == d_model
    _, kb, d_rnn = cbw.shape
    halo_a, halo_b = (ka - 1) * batch, (kb - 1) * batch
    scratch = (
        ((rows, d_model), _BF16),
        ((rows, n_groups * V7X_LANES), _BF16),
        ((rows, d_rnn), _BF16),
        ((n_groups, halo_a + rows, V7X_LANES), _F32),
        ((n_groups, rows, V7X_LANES), _F32),
        ((wside.shape[1], halo_b + rows, V7X_MXU_DIM), _F32),
        ((rows, d_rnn), _F32),
        ((rows, d_rnn), _F32),
        ((rows, d_rnn), _F32),
        ((rows, d_model), _F32),
        ((batch, d_rnn), _F32),
    )
    limit = _vmem_limit(weights, rows * d_model * 4, _scratch_bytes(scratch), 6 * rows * d_rnn * 4)
    row_spec = pl.BlockSpec((rows, d_model), lambda i: (i, 0))
    return pl.pallas_call(
        functools.partial(_mixer_kernel, batch=batch, groups=groups),
        grid=(n_rows // rows,),
        in_specs=[row_spec] + [_layer_resident(w, layer) for w in weights],
        out_specs=row_spec,
        out_shape=jax.ShapeDtypeStruct(x.shape, x.dtype),
        scratch_shapes=[pltpu.VMEM(s, d) for s, d in scratch],
        compiler_params=pltpu.CompilerParams(dimension_semantics=("arbitrary",), vmem_limit_bytes=limit),
        name="mixer",
    )(x, *weights)


def _mlp(x, layer, weights, *, final):
    n_rows, d_model = x.shape
    rows = MLP_ROWS
    scratch = (((rows, d_model), _BF16),)
    limit = _vmem_limit(weights, rows * d_model * 4, _scratch_bytes(scratch), 4 * rows * FF_BLOCK * 4)
    row_spec = pl.BlockSpec((rows, d_model), lambda i: (i, 0))
    return pl.pallas_call(
        functools.partial(_mlp_kernel, final=final),
        grid=(n_rows // rows,),
        in_specs=[row_spec] + [_layer_resident(w, layer) for w in weights],
        out_specs=row_spec,
        out_shape=jax.ShapeDtypeStruct(x.shape, x.dtype),
        scratch_shapes=[pltpu.VMEM(s, d) for s, d in scratch],
        compiler_params=pltpu.CompilerParams(dimension_semantics=("arbitrary",), vmem_limit_bytes=limit),
        name="mlp",
    )(x, *weights)


def kernel(x, g_mix, w_in, b_in, conv_a_w, conv_a_b, ln_g, ln_b, w_a_out, conv_b_w, conv_b_b, w_rg_a, b_rg_a,
           w_rg_x, b_rg_x, lam, w_b_out, w_o, g_mlp, w_1, w_2, g_final):
    batch, seq, d_model = x.shape
    depth = w_in.shape[0]
    d_conv = conv_a_w.shape[2]
    n_heads, head_dim = w_rg_a.shape[1], w_rg_a.shape[2]
    assert seq % MIXER_STEPS == 0 and (batch * seq) % MLP_ROWS == 0
    assert (n_heads * head_dim) % V7X_MXU_DIM == 0 and batch % 16 == 0
    assert d_conv % V7X_MXU_DIM == 0 and d_model % COL_BLOCK == 0
    groups = _gate_groups(n_heads, head_dim)

    def rows_of(a):
        return a[:, None, :]

    bias = rows_of(b_in)
    mixer_weights = (
        rows_of(g_mix),
        _glu_blocks(w_in[:, :, :d_conv], w_in[:, :, d_conv:2 * d_conv]).astype(_BF16),
        _glu_blocks(bias[:, :, :d_conv], bias[:, :, d_conv:2 * d_conv]),
        _col_blocks(w_in[:, :, 2 * d_conv:], V7X_MXU_DIM).astype(_BF16),
        _col_blocks(bias[:, :, 2 * d_conv:], V7X_MXU_DIM),
        _col_blocks(conv_a_w, V7X_LANES),
        _col_blocks(rows_of(conv_a_b), V7X_LANES),
        rows_of(ln_g),
        rows_of(ln_b),
        _col_blocks(w_a_out, V7X_MXU_DIM).astype(_BF16),
        conv_b_w,
        rows_of(conv_b_b),
        jax.vmap(functools.partial(_pack_gates, groups=groups))(w_rg_a, w_rg_x).astype(_BF16),
        jnp.stack([b_rg_a, b_rg_x], axis=1),
        rows_of(lam),
        _col_blocks(w_b_out, COL_BLOCK).astype(_BF16),
        _col_blocks(w_o, COL_BLOCK).astype(_BF16),
    )
    mlp_weights = (rows_of(g_mlp), w_1.astype(_BF16), w_2.astype(_BF16),
                   jnp.broadcast_to(g_final[None, None, :], (depth, 1, d_model)))

    xt = jnp.transpose(x, (1, 0, 2)).reshape(seq * batch, d_model)
    for l in range(depth):
        xt = _mixer(xt, l, mixer_weights, batch=batch, groups=groups)
        xt = _mlp(xt, l, mlp_weights, final=(l == depth - 1))
    return jnp.transpose(xt.reshape(seq, batch, d_model), (1, 0, 2))
```

```python
import functools

import jax
import jax.numpy as jnp
from jax import lax
from jax.experimental import pallas as pl
from jax.experimental.pallas import tpu as pltpu

LRU_C = 8.0
EPS = 1e-6

V7X_LANES = 128
V7X_MXU_DIM = 256
V7X_VMEM_BYTES = 64 * 1024 * 1024

MIXER_STEPS = 16
MLP_ROWS = 512
FF_BLOCK = 1024
CONV_GROUP = 4

_BF16 = jnp.bfloat16
_F32 = jnp.float32


def _dot(a, b):
    return lax.dot_general(a, b, (((1,), (0,)), ((), ())), preferred_element_type=_F32)


def _rms(x, g):
    return x * lax.rsqrt(jnp.mean(x * x, axis=-1, keepdims=True) + EPS) * g


def _sigmoid(x):
    return 0.5 * jnp.tanh(0.5 * x) + 0.5


def _band_ranges(n_heads, head_dim):
    d = n_heads * head_dim
    out = []
    for c0 in range(0, d, V7X_MXU_DIM):
        c1 = c0 + V7X_MXU_DIM
        k0 = (c0 // head_dim) * head_dim
        k1 = ((c1 - 1) // head_dim + 1) * head_dim
        k0 = k0 // V7X_LANES * V7X_LANES
        k1 = min(-(-k1 // V7X_LANES) * V7X_LANES, d)
        out.append((k0, k1))
    return tuple(out)


def _pack_gates(w_a, w_x, ranges):
    hd = w_a.shape[2]
    parts = []
    for j, (k0, k1) in enumerate(ranges):
        c0, c1 = j * V7X_MXU_DIM, (j + 1) * V7X_MXU_DIM
        halves = []
        for w in (w_a, w_x):
            tile = 0.0
            for h in range(c0 // hd, (c1 - 1) // hd + 1):
                lo, hi = max(h * hd, c0), min((h + 1) * hd, c1)
                piece = w[:, h, :, lo - h * hd:hi - h * hd]
                tile = tile + jnp.pad(piece, ((0, 0), (h * hd - k0, k1 - (h + 1) * hd), (lo - c0, c1 - hi)))
            halves.append(tile)
        parts.append(jnp.concatenate(halves, axis=2))
    return jnp.concatenate(parts, axis=1)


def _col_blocks(w, n):
    l, k, _ = w.shape
    return jnp.transpose(w.reshape(l, k, -1, n), (0, 2, 1, 3))


def _mixer_kernel(x_ref, g_ref, win_ref, bin_ref, caw_ref, cab_ref, lng_ref, lnb_ref,
                  waout_ref, cbw_ref, cbb_ref, wgate_ref, brg_ref, lam_ref, wbout_ref, wo_ref,
                  out_ref,
                  hbuf, actbuf, lhsbuf, ubuf, cbuf, pbuf, zbuf, gbuf, vbuf, abuf, bbuf, mbuf, hstate,
                  *, batch, ranges):
    rows, d_model = x_ref.shape
    n_groups, ka, _ = caw_ref.shape
    kb, d_rnn = cbw_ref.shape
    d_in = bin_ref.shape[1]
    d_conv = n_groups * V7X_LANES
    halo_a = (ka - 1) * batch
    halo_b = (kb - 1) * batch
    o_ga, o_xb = d_conv, 2 * d_conv
    o_gb = o_xb + d_rnn
    o_sa = o_gb + d_rnn
    o_sb = o_sa + d_model
    n_xb = d_rnn // V7X_MXU_DIM
    pid = pl.program_id(0)

    @pl.when(pid == 0)
    def _():
        ubuf[:, 0:halo_a, :] = jnp.zeros((n_groups, halo_a, V7X_LANES), _BF16)
        zbuf[:, 0:halo_b, :] = jnp.zeros((n_xb, halo_b, V7X_MXU_DIM), _F32)
        hstate[...] = jnp.zeros(hstate.shape, _F32)

    def conv_group(g):
        acc = jnp.broadcast_to(cab_ref[g], (rows, V7X_LANES))
        for k0 in range(0, ka, CONV_GROUP):
            part = None
            for k in range(k0, min(k0 + CONV_GROUP, ka)):
                tap = caw_ref[g, k:k + 1, :] * ubuf[g, k * batch:k * batch + rows, :]
                part = tap if part is None else part + tap
            acc = acc + part.astype(_F32)
        cbuf[g] = acc

    hbuf[...] = _rms(x_ref[...], g_ref[...]).astype(_BF16)
    pbuf[...] = _dot(hbuf[...], win_ref[:, :d_in]) + bin_ref[...]

    for g in range(n_groups):
        c = g * V7X_LANES
        glu = pbuf[:, c:c + V7X_LANES] * _sigmoid(pbuf[:, o_ga + c:o_ga + c + V7X_LANES])
        ubuf[g, halo_a:halo_a + rows, :] = glu.astype(_BF16)
    for i in range(n_xb):
        c = o_xb + i * V7X_MXU_DIM
        zbuf[i, halo_b:halo_b + rows, :] = pbuf[:, c:c + V7X_MXU_DIM]

    def conv_body(g, carry):
        conv_group(g)
        return carry

    lax.fori_loop(0, n_groups, conv_body, 0)

    for i in range(n_xb):
        c = i * V7X_MXU_DIM
        v = jnp.broadcast_to(cbb_ref[:, c:c + V7X_MXU_DIM], (rows, V7X_MXU_DIM))
        for k in range(kb):
            v = v + cbw_ref[k:k + 1, c:c + V7X_MXU_DIM] * zbuf[i, k * batch:k * batch + rows, :]
        for half in range(V7X_MXU_DIM // V7X_LANES):
            vbuf[2 * i + half] = v[:, half * V7X_LANES:(half + 1) * V7X_LANES]
        lhsbuf[:, c:c + V7X_MXU_DIM] = v.astype(_BF16)
    cv = jnp.concatenate([cbuf[g] for g in range(n_groups)], axis=1)
    mu = jnp.mean(cv, axis=-1, keepdims=True)
    xc = cv - mu
    yn = xc * lax.rsqrt(jnp.mean(xc * xc, axis=-1, keepdims=True) + EPS) * lng_ref[...] + lnb_ref[...]
    actbuf[...] = (yn * _sigmoid(yn)).astype(_BF16)

    off = 0
    for j, (k0, k1) in enumerate(ranges):
        z = _dot(lhsbuf[:, k0:k1], wgate_ref[off:off + (k1 - k0), :])
        off += k1 - k0
        for half in range(V7X_MXU_DIM // V7X_LANES):
            c = half * V7X_LANES
            gbuf[2 * j + half] = jnp.concatenate(
                [z[:, c:c + V7X_LANES], z[:, V7X_MXU_DIM + c:V7X_MXU_DIM + c + V7X_LANES]], axis=1)
    y_a = _dot(actbuf[...], waout_ref[:, :d_model])
    mbuf[...] = _sigmoid(pbuf[:, o_sa:o_sa + d_model]) * y_a

    row = pid * rows + lax.broadcasted_iota(jnp.int32, (rows, 1), 0)
    is_start = row < batch

    def gate_body(q, carry):
        gates = gbuf[q]
        bias = brg_ref[q]
        x_lam = -lam_ref[q]
        softplus_neg_lam = jnp.maximum(x_lam, 0.0) + jnp.log1p(jnp.exp(-jnp.abs(x_lam)))
        i_gate = _sigmoid(gates[:, V7X_LANES:] + bias[1:2, :])
        tanh_r = jnp.tanh(0.5 * (gates[:, :V7X_LANES] + bias[0:1, :]))
        log_a = (tanh_r + 1.0) * ((-0.5 * LRU_C) * softplus_neg_lam)
        a = jnp.exp(log_a)
        s = jnp.tanh(-log_a) * (1.0 + a * a)
        mult = jnp.where(s > 0.0, s * lax.rsqrt(s), 0.0)
        mult = jnp.where(is_start, 1.0, mult)
        abuf[q] = a
        bbuf[q] = mult * i_gate * vbuf[q]
        return carry

    lax.fori_loop(0, gbuf.shape[0], gate_body, 0)

    h = hstate[...]
    for t in range(rows // batch):
        r0 = t * batch
        h = abuf[:, r0:r0 + batch, :] * h + bbuf[:, r0:r0 + batch, :]
        bbuf[:, r0:r0 + batch, :] = h
    hstate[...] = h

    for i in range(n_xb):
        c = i * V7X_MXU_DIM
        state = jnp.concatenate([bbuf[2 * i], bbuf[2 * i + 1]], axis=1)
        lhsbuf[:, c:c + V7X_MXU_DIM] = (state * jax.nn.gelu(pbuf[:, o_gb + c:o_gb + c + V7X_MXU_DIM])).astype(_BF16)
    y_b = _dot(lhsbuf[...], wbout_ref[:, :d_model])
    actbuf[...] = (mbuf[...] + _sigmoid(pbuf[:, o_sb:o_sb + d_model]) * y_b).astype(_BF16)
    out_ref[...] = x_ref[...] + _dot(actbuf[...], wo_ref[:, :d_model])

    for s0 in range(0, halo_a, rows):
        n = min(rows, halo_a - s0)
        ubuf[:, s0:s0 + n, :] = ubuf[:, rows + s0:rows + s0 + n, :]
    for s0 in range(0, halo_b, rows):
        n = min(rows, halo_b - s0)
        zbuf[:, s0:s0 + n, :] = zbuf[:, rows + s0:rows + s0 + n, :]


def _mlp_kernel(x_ref, g_ref, w1_ref, w2_ref, gfin_ref, out_ref, hbuf, *, final):
    d_ff = w1_ref.shape[1]
    x = x_ref[...]
    hbuf[...] = _rms(x, g_ref[...]).astype(_BF16)
    acc = x
    for c in range(0, d_ff, FF_BLOCK):
        f = _dot(hbuf[...], w1_ref[:, c:c + FF_BLOCK])
        f = jnp.square(jnp.maximum(f, 0.0)).astype(_BF16)
        acc = acc + _dot(f, w2_ref[c:c + FF_BLOCK, :])
    if final:
        acc = _rms(acc, gfin_ref[...])
    out_ref[...] = acc


def _layer_resident(a, layer):
    zeros = (0,) * (a.ndim - 1)
    return pl.BlockSpec((None,) + a.shape[1:], lambda i: (layer,) + zeros, pipeline_mode=pl.Buffered(1))


def _layer_bytes(*arrays):
    return sum(a.size // a.shape[0] * a.dtype.itemsize for a in arrays)


def _scratch_bytes(shapes):
    return sum(functools.reduce(lambda a, b: a * b, s) * jnp.dtype(d).itemsize for s, d in shapes)


def _vmem_limit(weights, row_block_bytes, scratch_bytes, temp_bytes):
    want = _layer_bytes(*weights) + 2 * 2 * row_block_bytes + scratch_bytes + temp_bytes
    return min(want, V7X_VMEM_BYTES)


def _mixer(x, layer, weights, *, batch, ranges):
    n_rows, d_model = x.shape
    rows = MIXER_STEPS * batch
    b_in, caw, cbw = weights[2], weights[3], weights[8]
    _, n_groups, ka, _ = caw.shape
    assert n_groups * V7X_LANES == d_model
    _, kb, d_rnn = cbw.shape
    halo_a, halo_b = (ka - 1) * batch, (kb - 1) * batch
    n_gate = d_rnn // V7X_LANES
    scratch = (
        ((rows, d_model), _BF16),
        ((rows, n_groups * V7X_LANES), _BF16),
        ((rows, d_rnn), _BF16),
        ((n_groups, halo_a + rows, V7X_LANES), _BF16),
        ((n_groups, rows, V7X_LANES), _F32),
        ((rows, b_in.shape[2]), _F32),
        ((d_rnn // V7X_MXU_DIM, halo_b + rows, V7X_MXU_DIM), _F32),
        ((n_gate, rows, 2 * V7X_LANES), _F32),
        ((n_gate, rows, V7X_LANES), _F32),
        ((n_gate, rows, V7X_LANES), _F32),
        ((n_gate, rows, V7X_LANES), _F32),
        ((rows, d_model), _F32),
        ((n_gate, batch, V7X_LANES), _F32),
    )
    limit = _vmem_limit(weights, rows * d_model * 4, _scratch_bytes(scratch), 6 * rows * d_rnn * 4)
    row_spec = pl.BlockSpec((rows, d_model), lambda i: (i, 0))
    return pl.pallas_call(
        functools.partial(_mixer_kernel, batch=batch, ranges=ranges),
        grid=(n_rows // rows,),
        in_specs=[row_spec] + [_layer_resident(w, layer) for w in weights],
        out_specs=row_spec,
        out_shape=jax.ShapeDtypeStruct(x.shape, x.dtype),
        scratch_shapes=[pltpu.VMEM(s, d) for s, d in scratch],
        compiler_params=pltpu.CompilerParams(dimension_semantics=("arbitrary",), vmem_limit_bytes=limit),
        name="mixer",
    )(x, *weights)


def _mlp(x, layer, weights, *, final):
    n_rows, d_model = x.shape
    rows = MLP_ROWS
    scratch = (((rows, d_model), _BF16),)
    limit = _vmem_limit(weights, rows * d_model * 4, _scratch_bytes(scratch), 4 * rows * FF_BLOCK * 4)
    row_spec = pl.BlockSpec((rows, d_model), lambda i: (i, 0))
    return pl.pallas_call(
        functools.partial(_mlp_kernel, final=final),
        grid=(n_rows // rows,),
        in_specs=[row_spec] + [_layer_resident(w, layer) for w in weights],
        out_specs=row_spec,
        out_shape=jax.ShapeDtypeStruct(x.shape, x.dtype),
        scratch_shapes=[pltpu.VMEM(s, d) for s, d in scratch],
        compiler_params=pltpu.CompilerParams(dimension_semantics=("arbitrary",), vmem_limit_bytes=limit),
        name="mlp",
    )(x, *weights)


def kernel(x, g_mix, w_in, b_in, conv_a_w, conv_a_b, ln_g, ln_b, w_a_out, conv_b_w, conv_b_b, w_rg_a, b_rg_a,
           w_rg_x, b_rg_x, lam, w_b_out, w_o, g_mlp, w_1, w_2, g_final):
    batch, seq, d_model = x.shape
    depth = w_in.shape[0]
    d_conv = conv_a_w.shape[2]
    n_heads, head_dim = w_rg_a.shape[1], w_rg_a.shape[2]
    assert seq % MIXER_STEPS == 0 and (batch * seq) % MLP_ROWS == 0
    assert (n_heads * head_dim) % V7X_MXU_DIM == 0 and batch % 16 == 0
    assert d_conv % V7X_MXU_DIM == 0
    ranges = _band_ranges(n_heads, head_dim)

    def rows_of(a):
        return a[:, None, :]

    def matmul_weight(w):
        w = w.astype(_BF16)
        if (w.shape[2] // V7X_LANES) % 8 == 0:
            w = jnp.pad(w, ((0, 0), (0, 0), (0, V7X_LANES)))
        return w

    mixer_weights = (
        rows_of(g_mix),
        matmul_weight(w_in),
        rows_of(b_in),
        _col_blocks(conv_a_w.astype(_BF16), V7X_LANES),
        _col_blocks(rows_of(conv_a_b), V7X_LANES),
        rows_of(ln_g),
        rows_of(ln_b),
        matmul_weight(w_a_out),
        conv_b_w,
        rows_of(conv_b_b),
        _pack_gates(w_rg_a, w_rg_x, ranges).astype(_BF16),
        _col_blocks(jnp.stack([b_rg_a, b_rg_x], axis=1), V7X_LANES),
        _col_blocks(rows_of(lam), V7X_LANES),
        matmul_weight(w_b_out),
        matmul_weight(w_o),
    )
    mlp_weights = (rows_of(g_mlp), w_1.astype(_BF16), w_2.astype(_BF16),
                   jnp.broadcast_to(g_final[None, None, :], (depth, 1, d_model)))

    xt = jnp.transpose(x, (1, 0, 2)).reshape(seq * batch, d_model)
    for l in range(depth):
        xt = _mixer(xt, l, mixer_weights, batch=batch, ranges=ranges)
        xt = _mlp(xt, l, mlp_weights, final=(l == depth - 1))
    return jnp.transpose(xt.reshape(seq, batch, d_model), (1, 0, 2))
```

```python
import functools

import jax
import jax.numpy as jnp
from jax import lax
from jax.experimental import pallas as pl
from jax.experimental.pallas import tpu as pltpu

LRU_C = 8.0
EPS = 1e-6

V7X_LANES = 128
V7X_MXU_DIM = 256
V7X_VMEM_BYTES = 64 * 1024 * 1024

MIXER_STEPS = 16
MLP_ROWS = 512
FF_BLOCK = 1024
CONV_GROUP = 4

_BF16 = jnp.bfloat16
_F32 = jnp.float32


def _dot(a, b):
    return lax.dot_general(a, b, (((1,), (0,)), ((), ())), preferred_element_type=_F32)


def _rms(x, g):
    return x * lax.rsqrt(jnp.mean(x * x, axis=-1, keepdims=True) + EPS) * g


def _sigmoid(x):
    return 0.5 * jnp.tanh(0.5 * x) + 0.5


def _gelu_tanh(x):
    c0 = 0.7978845608028654
    z = x * (x * x * (c0 * 0.044715) + c0)
    half = 0.5 * x
    return half * jnp.tanh(z) + half


def _band_ranges(n_heads, head_dim):
    d = n_heads * head_dim
    out = []
    for c0 in range(0, d, V7X_MXU_DIM):
        c1 = c0 + V7X_MXU_DIM
        k0 = (c0 // head_dim) * head_dim
        k1 = ((c1 - 1) // head_dim + 1) * head_dim
        k0 = k0 // V7X_LANES * V7X_LANES
        k1 = min(-(-k1 // V7X_LANES) * V7X_LANES, d)
        out.append((k0, k1))
    return tuple(out)


def _pack_gates(w_a, w_x, ranges):
    hd = w_a.shape[2]
    parts = []
    for j, (k0, k1) in enumerate(ranges):
        c0, c1 = j * V7X_MXU_DIM, (j + 1) * V7X_MXU_DIM
        halves = []
        for w in (w_a, w_x):
            tile = 0.0
            for h in range(c0 // hd, (c1 - 1) // hd + 1):
                lo, hi = max(h * hd, c0), min((h + 1) * hd, c1)
                piece = w[:, h, :, lo - h * hd:hi - h * hd]
                tile = tile + jnp.pad(piece, ((0, 0), (h * hd - k0, k1 - (h + 1) * hd), (lo - c0, c1 - hi)))
            halves.append(tile)
        parts.append(jnp.concatenate(halves, axis=2))
    return jnp.concatenate(parts, axis=1)


def _col_blocks(w, n):
    l, k, _ = w.shape
    return jnp.transpose(w.reshape(l, k, -1, n), (0, 2, 1, 3))


def _mixer_kernel(x_ref, g_ref, win_ref, bin_ref, caw_ref, cab_ref, lng_ref, lnb_ref,
                  waout_ref, cbw_ref, cbb_ref, wgate_ref, brg_ref, lam_ref, wbout_ref, wo_ref,
                  out_ref,
                  hbuf, actbuf, lhsbuf, ubuf, cbuf, pbuf, zbuf, gbuf, vbuf, abuf, bbuf, mbuf, hstate,
                  *, batch, ranges):
    rows, d_model = out_ref.shape
    n_groups, ka, _ = caw_ref.shape
    kb, d_rnn = cbw_ref.shape
    d_in = bin_ref.shape[1]
    d_conv = n_groups * V7X_LANES
    halo_a = (ka - 1) * batch
    halo_b = (kb - 1) * batch
    o_ga, o_xb = d_conv, 2 * d_conv
    o_gb = o_xb + d_rnn
    o_sa = o_gb + d_rnn
    o_sb = o_sa + d_model
    n_xb = d_rnn // V7X_MXU_DIM
    pid = pl.program_id(0)

    @pl.when(pid == 0)
    def _():
        ubuf[:, 0:halo_a, :] = jnp.zeros((n_groups, halo_a, V7X_LANES), _BF16)
        zbuf[:, 0:halo_b, :] = jnp.zeros((n_xb, halo_b, V7X_MXU_DIM), _F32)
        hstate[...] = jnp.zeros(hstate.shape, _F32)

    def conv_group(g):
        acc = jnp.broadcast_to(cab_ref[g], (rows, V7X_LANES))
        for k0 in range(0, ka, CONV_GROUP):
            part = None
            for k in range(k0, min(k0 + CONV_GROUP, ka)):
                tap = caw_ref[g, k:k + 1, :] * ubuf[g, k * batch:k * batch + rows, :]
                part = tap if part is None else part + tap
            acc = acc + part.astype(_F32)
        cbuf[g] = acc

    if len(x_ref.shape) == 3:
        x_rows = jnp.swapaxes(x_ref[...], 0, 1).reshape(rows, d_model)
    else:
        x_rows = x_ref[...]
    hbuf[...] = _rms(x_rows, g_ref[...]).astype(_BF16)
    pbuf[...] = _dot(hbuf[...], win_ref[:, :d_in]) + bin_ref[...]

    for g in range(n_groups):
        c = g * V7X_LANES
        glu = pbuf[:, c:c + V7X_LANES] * _sigmoid(pbuf[:, o_ga + c:o_ga + c + V7X_LANES])
        ubuf[g, halo_a:halo_a + rows, :] = glu.astype(_BF16)
    for i in range(n_xb):
        c = i * V7X_MXU_DIM
        zbuf[i, halo_b:halo_b + rows, :] = pbuf[:, o_xb + c:o_xb + c + V7X_MXU_DIM]
        v = jnp.broadcast_to(cbb_ref[:, c:c + V7X_MXU_DIM], (rows, V7X_MXU_DIM))
        for k in range(kb):
            v = v + cbw_ref[k:k + 1, c:c + V7X_MXU_DIM] * zbuf[i, k * batch:k * batch + rows, :]
        for half in range(V7X_MXU_DIM // V7X_LANES):
            vbuf[2 * i + half] = v[:, half * V7X_LANES:(half + 1) * V7X_LANES]
        lhsbuf[:, c:c + V7X_MXU_DIM] = v.astype(_BF16)
    for c in range(o_gb, o_sa, V7X_MXU_DIM):
        pbuf[:, c:c + V7X_MXU_DIM] = _gelu_tanh(pbuf[:, c:c + V7X_MXU_DIM])
    for c in range(o_sa, d_in, V7X_MXU_DIM):
        pbuf[:, c:c + V7X_MXU_DIM] = _sigmoid(pbuf[:, c:c + V7X_MXU_DIM])

    for g in range(n_groups):
        conv_group(g)
    cv = jnp.concatenate([cbuf[g] for g in range(n_groups)], axis=1)
    mu = jnp.mean(cv, axis=-1, keepdims=True)
    xc = cv - mu
    yn = xc * lax.rsqrt(jnp.mean(xc * xc, axis=-1, keepdims=True) + EPS) * lng_ref[...] + lnb_ref[...]
    actbuf[...] = (yn * _sigmoid(yn)).astype(_BF16)

    off = 0
    for j, (k0, k1) in enumerate(ranges):
        z = _dot(lhsbuf[:, k0:k1], wgate_ref[off:off + (k1 - k0), :])
        off += k1 - k0
        for half in range(V7X_MXU_DIM // V7X_LANES):
            c = half * V7X_LANES
            gbuf[2 * j + half] = jnp.concatenate(
                [z[:, c:c + V7X_LANES], z[:, V7X_MXU_DIM + c:V7X_MXU_DIM + c + V7X_LANES]], axis=1)
    y_a = _dot(actbuf[...], waout_ref[:, :d_model])
    mbuf[...] = pbuf[:, o_sa:o_sa + d_model] * y_a

    row = pid * rows + lax.broadcasted_iota(jnp.int32, (rows, 1), 0)
    is_start = row < batch

    def gate_body(q, carry):
        gates = gbuf[q]
        bias = brg_ref[q]
        x_lam = -lam_ref[q]
        softplus_neg_lam = jnp.maximum(x_lam, 0.0) + jnp.log1p(jnp.exp(-jnp.abs(x_lam)))
        i_gate = _sigmoid(gates[:, V7X_LANES:] + bias[1:2, :])
        tanh_r = jnp.tanh(0.5 * (gates[:, :V7X_LANES] + bias[0:1, :]))
        log_a = (tanh_r + 1.0) * ((-0.5 * LRU_C) * softplus_neg_lam)
        a = jnp.exp(log_a)
        s = jnp.tanh(-log_a) * (1.0 + a * a)
        mult = jnp.where(s > 0.0, s * lax.rsqrt(s), 0.0)
        mult = jnp.where(is_start, 1.0, mult)
        abuf[q] = a
        bbuf[q] = mult * i_gate * vbuf[q]
        return carry

    for q in range(gbuf.shape[0]):
        gate_body(q, 0)

    h = hstate[...]
    for t in range(rows // batch):
        r0 = t * batch
        h = abuf[:, r0:r0 + batch, :] * h + bbuf[:, r0:r0 + batch, :]
        bbuf[:, r0:r0 + batch, :] = h
    hstate[...] = h

    for i in range(n_xb):
        c = i * V7X_MXU_DIM
        state = jnp.concatenate([bbuf[2 * i], bbuf[2 * i + 1]], axis=1)
        lhsbuf[:, c:c + V7X_MXU_DIM] = (state * pbuf[:, o_gb + c:o_gb + c + V7X_MXU_DIM]).astype(_BF16)
    y_b = _dot(lhsbuf[...], wbout_ref[:, :d_model])
    actbuf[...] = (mbuf[...] + pbuf[:, o_sb:o_sb + d_model] * y_b).astype(_BF16)
    out_ref[...] = x_rows + _dot(actbuf[...], wo_ref[:, :d_model])

    for s0 in range(0, halo_a, rows):
        n = min(rows, halo_a - s0)
        ubuf[:, s0:s0 + n, :] = ubuf[:, rows + s0:rows + s0 + n, :]
    for s0 in range(0, halo_b, rows):
        n = min(rows, halo_b - s0)
        zbuf[:, s0:s0 + n, :] = zbuf[:, rows + s0:rows + s0 + n, :]


def _mlp_kernel(x_ref, g_ref, w1_ref, w2_ref, gfin_ref, out_ref, hbuf, *, final):
    d_ff = w1_ref.shape[1]
    x = x_ref[...]
    hbuf[...] = _rms(x, g_ref[...]).astype(_BF16)
    acc = x
    for c in range(0, d_ff, FF_BLOCK):
        f = _dot(hbuf[...], w1_ref[:, c:c + FF_BLOCK])
        f = jnp.square(jnp.maximum(f, 0.0)).astype(_BF16)
        acc = acc + _dot(f, w2_ref[c:c + FF_BLOCK, :])
    if final:
        acc = _rms(acc, gfin_ref[...])
        batch = out_ref.shape[0]
        out_ref[...] = jnp.swapaxes(acc.reshape(acc.shape[0] // batch, batch, acc.shape[1]), 0, 1)
    else:
        out_ref[...] = acc


def _layer_resident(a, layer):
    zeros = (0,) * (a.ndim - 1)
    return pl.BlockSpec((None,) + a.shape[1:], lambda i: (layer,) + zeros, pipeline_mode=pl.Buffered(1))


def _layer_bytes(*arrays):
    return sum(a.size // a.shape[0] * a.dtype.itemsize for a in arrays)


def _scratch_bytes(shapes):
    return sum(functools.reduce(lambda a, b: a * b, s) * jnp.dtype(d).itemsize for s, d in shapes)


def _vmem_limit(weights, row_block_bytes, scratch_bytes, temp_bytes):
    want = _layer_bytes(*weights) + 2 * 2 * row_block_bytes + scratch_bytes + temp_bytes
    return min(want, V7X_VMEM_BYTES)


def _mixer(x, layer, weights, *, batch, ranges):
    d_model = x.shape[-1]
    n_rows = x.size // d_model
    rows = MIXER_STEPS * batch
    b_in, caw, cbw = weights[2], weights[3], weights[8]
    _, n_groups, ka, _ = caw.shape
    assert n_groups * V7X_LANES == d_model
    _, kb, d_rnn = cbw.shape
    halo_a, halo_b = (ka - 1) * batch, (kb - 1) * batch
    n_gate = d_rnn // V7X_LANES
    scratch = (
        ((rows, d_model), _BF16),
        ((rows, n_groups * V7X_LANES), _BF16),
        ((rows, d_rnn), _BF16),
        ((n_groups, halo_a + rows, V7X_LANES), _BF16),
        ((n_groups, rows, V7X_LANES), _F32),
        ((rows, b_in.shape[2]), _F32),
        ((d_rnn // V7X_MXU_DIM, halo_b + rows, V7X_MXU_DIM), _F32),
        ((n_gate, rows, 2 * V7X_LANES), _F32),
        ((n_gate, rows, V7X_LANES), _F32),
        ((n_gate, rows, V7X_LANES), _F32),
        ((n_gate, rows, V7X_LANES), _F32),
        ((rows, d_model), _F32),
        ((n_gate, batch, V7X_LANES), _F32),
    )
    limit = _vmem_limit(weights, rows * d_model * 4, _scratch_bytes(scratch), 6 * rows * d_rnn * 4)
    row_spec = pl.BlockSpec((rows, d_model), lambda i: (i, 0))
    x_spec = row_spec if x.ndim == 2 else pl.BlockSpec((batch, MIXER_STEPS, d_model), lambda i: (0, i, 0))
    return pl.pallas_call(
        functools.partial(_mixer_kernel, batch=batch, ranges=ranges),
        grid=(n_rows // rows,),
        in_specs=[x_spec] + [_layer_resident(w, layer) for w in weights],
        out_specs=row_spec,
        out_shape=jax.ShapeDtypeStruct((n_rows, d_model), x.dtype),
        scratch_shapes=[pltpu.VMEM(s, d) for s, d in scratch],
        compiler_params=pltpu.CompilerParams(dimension_semantics=("arbitrary",), vmem_limit_bytes=limit),
        name="mixer",
    )(x, *weights)


def _mlp(x, layer, weights, *, batch, final):
    n_rows, d_model = x.shape
    rows = MLP_ROWS
    scratch = (((rows, d_model), _BF16),)
    limit = _vmem_limit(weights, rows * d_model * 4, _scratch_bytes(scratch), 4 * rows * FF_BLOCK * 4)
    row_spec = pl.BlockSpec((rows, d_model), lambda i: (i, 0))
    return pl.pallas_call(
        functools.partial(_mlp_kernel, final=final),
        grid=(n_rows // rows,),
        in_specs=[row_spec] + [_layer_resident(w, layer) for w in weights],
        out_specs=pl.BlockSpec((batch, rows // batch, d_model), lambda i: (0, i, 0)) if final else row_spec,
        out_shape=jax.ShapeDtypeStruct((batch, n_rows // batch, d_model) if final else x.shape, x.dtype),
        scratch_shapes=[pltpu.VMEM(s, d) for s, d in scratch],
        compiler_params=pltpu.CompilerParams(dimension_semantics=("arbitrary",), vmem_limit_bytes=limit),
        name="mlp",
    )(x, *weights)


def kernel(x, g_mix, w_in, b_in, conv_a_w, conv_a_b, ln_g, ln_b, w_a_out, conv_b_w, conv_b_b, w_rg_a, b_rg_a,
           w_rg_x, b_rg_x, lam, w_b_out, w_o, g_mlp, w_1, w_2, g_final):
    batch, seq, d_model = x.shape
    depth = w_in.shape[0]
    d_conv = conv_a_w.shape[2]
    n_heads, head_dim = w_rg_a.shape[1], w_rg_a.shape[2]
    assert seq % MIXER_STEPS == 0 and (batch * seq) % MLP_ROWS == 0
    assert (n_heads * head_dim) % V7X_MXU_DIM == 0 and batch % 16 == 0
    assert d_conv % V7X_MXU_DIM == 0
    ranges = _band_ranges(n_heads, head_dim)

    def rows_of(a):
        return a[:, None, :]

    def matmul_weight(w):
        w = w.astype(_BF16)
        if (w.shape[2] // V7X_LANES) % 8 == 0:
            w = jnp.pad(w, ((0, 0), (0, 0), (0, V7X_LANES)))
        return w

    mixer_weights = (
        rows_of(g_mix),
        matmul_weight(w_in),
        rows_of(b_in),
        _col_blocks(conv_a_w.astype(_BF16), V7X_LANES),
        _col_blocks(rows_of(conv_a_b), V7X_LANES),
        rows_of(ln_g),
        rows_of(ln_b),
        matmul_weight(w_a_out),
        conv_b_w,
        rows_of(conv_b_b),
        _pack_gates(w_rg_a, w_rg_x, ranges).astype(_BF16),
        _col_blocks(jnp.stack([b_rg_a, b_rg_x], axis=1), V7X_LANES),
        _col_blocks(rows_of(lam), V7X_LANES),
        matmul_weight(w_b_out),
        matmul_weight(w_o),
    )
    mlp_weights = (rows_of(g_mlp), w_1.astype(_BF16), w_2.astype(_BF16),
                   jnp.broadcast_to(g_final[None, None, :], (depth, 1, d_model)))

    for l in range(depth):
        x = _mixer(x, l, mixer_weights, batch=batch, ranges=ranges)
        x = _mlp(x, l, mlp_weights, batch=batch, final=(l == depth - 1))
    return x
```

```python
import functools

import jax
import jax.numpy as jnp
from jax import lax
from jax.experimental import pallas as pl
from jax.experimental.pallas import tpu as pltpu

LRU_C = 8.0
EPS = 1e-6

V7X_LANES = 128
V7X_MXU_DIM = 256
V7X_VMEM_BYTES = 64 * 1024 * 1024

MIXER_STEPS = 16
MLP_ROWS = 512
FF_BLOCK = 1024
CONV_GROUP = 4

_BF16 = jnp.bfloat16
_F32 = jnp.float32


def _dot(a, b):
    return lax.dot_general(a, b, (((1,), (0,)), ((), ())), preferred_element_type=_F32)


def _rms(x, g):
    return x * lax.rsqrt(jnp.mean(x * x, axis=-1, keepdims=True) + EPS) * g


def _sigmoid(x):
    return 0.5 * jnp.tanh(0.5 * x) + 0.5


def _gelu_tanh(x):
    c0 = 0.7978845608028654
    z = x * (x * x * (c0 * 0.044715) + c0)
    half = 0.5 * x
    return half * jnp.tanh(z) + half


def _band_ranges(n_heads, head_dim):
    d = n_heads * head_dim
    out = []
    for c0 in range(0, d, V7X_MXU_DIM):
        c1 = c0 + V7X_MXU_DIM
        k0 = (c0 // head_dim) * head_dim
        k1 = ((c1 - 1) // head_dim + 1) * head_dim
        k0 = k0 // V7X_LANES * V7X_LANES
        k1 = min(-(-k1 // V7X_LANES) * V7X_LANES, d)
        out.append((k0, k1))
    return tuple(out)


def _pack_gates(w_a, w_x, ranges):
    hd = w_a.shape[2]
    parts = []
    for j, (k0, k1) in enumerate(ranges):
        c0, c1 = j * V7X_MXU_DIM, (j + 1) * V7X_MXU_DIM
        halves = []
        for w in (w_a, w_x):
            tile = 0.0
            for h in range(c0 // hd, (c1 - 1) // hd + 1):
                lo, hi = max(h * hd, c0), min((h + 1) * hd, c1)
                piece = w[:, h, :, lo - h * hd:hi - h * hd]
                tile = tile + jnp.pad(piece, ((0, 0), (h * hd - k0, k1 - (h + 1) * hd), (lo - c0, c1 - hi)))
            halves.append(tile)
        parts.append(jnp.concatenate(halves, axis=2))
    return jnp.concatenate(parts, axis=1)


def _col_blocks(w, n):
    l, k, _ = w.shape
    return jnp.transpose(w.reshape(l, k, -1, n), (0, 2, 1, 3))


def _mixer_kernel(x_ref, g_ref, win_ref, bin_ref, caw_ref, cab_ref, lng_ref, lnb_ref,
                  waout_ref, cbw_ref, cbb_ref, wgate_ref, brg_ref, lam_ref, wbout_ref, wo_ref,
                  out_ref,
                  hbuf, actbuf, lhsbuf, ubuf, cbuf, pbuf, zbuf, gbuf, vbuf, abuf, bbuf, mbuf, hstate,
                  *, batch, ranges):
    rows, d_model = out_ref.shape
    n_groups, ka, _ = caw_ref.shape
    kb, d_rnn = cbw_ref.shape
    d_in = bin_ref.shape[1]
    d_conv = n_groups * V7X_LANES
    halo_a = (ka - 1) * batch
    halo_b = (kb - 1) * batch
    o_ga, o_xb = d_conv, 2 * d_conv
    o_gb = o_xb + d_rnn
    o_sa = o_gb + d_rnn
    o_sb = o_sa + d_model
    n_xb = d_rnn // V7X_MXU_DIM
    pid = pl.program_id(0)

    @pl.when(pid == 0)
    def _():
        ubuf[:, 0:halo_a, :] = jnp.zeros((n_groups, halo_a, V7X_LANES), _BF16)
        zbuf[:, 0:halo_b, :] = jnp.zeros((n_xb, halo_b, V7X_MXU_DIM), _F32)
        hstate[...] = jnp.zeros(hstate.shape, _F32)

    def conv_group(g):
        acc = jnp.broadcast_to(cab_ref[g], (rows, V7X_LANES))
        for k0 in range(0, ka, CONV_GROUP):
            part = None
            for k in range(k0, min(k0 + CONV_GROUP, ka)):
                tap = caw_ref[g, k:k + 1, :] * ubuf[g, k * batch:k * batch + rows, :]
                part = tap if part is None else part + tap
            acc = acc + part.astype(_F32)
        cbuf[g] = acc

    if len(x_ref.shape) == 3:
        x_rows = jnp.swapaxes(x_ref[...], 0, 1).reshape(rows, d_model)
    else:
        x_rows = x_ref[...]
    hbuf[...] = _rms(x_rows, g_ref[...]).astype(_BF16)
    pbuf[...] = _dot(hbuf[...], win_ref[:, :d_in]) + bin_ref[...]

    for g in range(n_groups):
        c = g * V7X_LANES
        glu = pbuf[:, c:c + V7X_LANES] * _sigmoid(pbuf[:, o_ga + c:o_ga + c + V7X_LANES])
        ubuf[g, halo_a:halo_a + rows, :] = glu.astype(_BF16)

    for i in range(n_xb):
        c = i * V7X_MXU_DIM
        zbuf[i, halo_b:halo_b + rows, :] = pbuf[:, o_xb + c:o_xb + c + V7X_MXU_DIM]
        v = jnp.broadcast_to(cbb_ref[:, c:c + V7X_MXU_DIM], (rows, V7X_MXU_DIM))
        for k in range(kb):
            v = v + cbw_ref[k:k + 1, c:c + V7X_MXU_DIM] * zbuf[i, k * batch:k * batch + rows, :]
        for half in range(V7X_MXU_DIM // V7X_LANES):
            vbuf[2 * i + half] = v[:, half * V7X_LANES:(half + 1) * V7X_LANES]
        lhsbuf[:, c:c + V7X_MXU_DIM] = v.astype(_BF16)
    for c in range(o_gb, o_sa, V7X_MXU_DIM):
        pbuf[:, c:c + V7X_MXU_DIM] = _gelu_tanh(pbuf[:, c:c + V7X_MXU_DIM])
    for c in range(o_sa, d_in, V7X_MXU_DIM):
        pbuf[:, c:c + V7X_MXU_DIM] = _sigmoid(pbuf[:, c:c + V7X_MXU_DIM])

    for g in range(n_groups):
        conv_group(g)
    cv = jnp.concatenate([cbuf[g] for g in range(n_groups)], axis=1)
    mu = jnp.mean(cv, axis=-1, keepdims=True)
    xc = cv - mu
    yn = xc * lax.rsqrt(jnp.mean(xc * xc, axis=-1, keepdims=True) + EPS) * lng_ref[...] + lnb_ref[...]
    actbuf[...] = (yn * _sigmoid(yn)).astype(_BF16)

    off = 0
    for j, (k0, k1) in enumerate(ranges):
        z = _dot(lhsbuf[:, k0:k1], wgate_ref[off:off + (k1 - k0), :])
        off += k1 - k0
        for half in range(V7X_MXU_DIM // V7X_LANES):
            c = half * V7X_LANES
            gbuf[2 * j + half] = jnp.concatenate(
                [z[:, c:c + V7X_LANES], z[:, V7X_MXU_DIM + c:V7X_MXU_DIM + c + V7X_LANES]], axis=1)
    y_a = _dot(actbuf[...], waout_ref[:, :d_model])
    mbuf[...] = pbuf[:, o_sa:o_sa + d_model] * y_a

    row = pid * rows + lax.broadcasted_iota(jnp.int32, (rows, 1), 0)
    is_start = row < batch

    def gate_body(q, carry):
        gates = gbuf[q]
        bias = brg_ref[q]
        x_lam = -lam_ref[q]
        softplus_neg_lam = jnp.maximum(x_lam, 0.0) + jnp.log1p(jnp.exp(-jnp.abs(x_lam)))
        i_gate = _sigmoid(gates[:, V7X_LANES:] + bias[1:2, :])
        tanh_r = jnp.tanh(0.5 * (gates[:, :V7X_LANES] + bias[0:1, :]))
        log_a = (tanh_r + 1.0) * ((-0.5 * LRU_C) * softplus_neg_lam)
        a = jnp.exp(log_a)
        s = jnp.tanh(-log_a) * (1.0 + a * a)
        mult = jnp.where(s > 0.0, s * lax.rsqrt(s), 0.0)
        mult = jnp.where(is_start, 1.0, mult)
        abuf[q] = a
        bbuf[q] = mult * i_gate * vbuf[q]
        return carry

    for q in range(gbuf.shape[0]):
        gate_body(q, 0)

    h = hstate[...]
    for t in range(rows // batch):
        r0 = t * batch
        h = abuf[:, r0:r0 + batch, :] * h + bbuf[:, r0:r0 + batch, :]
        bbuf[:, r0:r0 + batch, :] = h
    hstate[...] = h

    for i in range(n_xb):
        c = i * V7X_MXU_DIM
        state = jnp.concatenate([bbuf[2 * i], bbuf[2 * i + 1]], axis=1)
        lhsbuf[:, c:c + V7X_MXU_DIM] = (state * pbuf[:, o_gb + c:o_gb + c + V7X_MXU_DIM]).astype(_BF16)
    y_b = _dot(lhsbuf[...], wbout_ref[:, :d_model])
    actbuf[...] = (mbuf[...] + pbuf[:, o_sb:o_sb + d_model] * y_b).astype(_BF16)
    out_ref[...] = x_rows + _dot(actbuf[...], wo_ref[:, :d_model])

    for s0 in range(0, halo_a, rows):
        n = min(rows, halo_a - s0)
        ubuf[:, s0:s0 + n, :] = ubuf[:, rows + s0:rows + s0 + n, :]
    for s0 in range(0, halo_b, rows):
        n = min(rows, halo_b - s0)
        zbuf[:, s0:s0 + n, :] = zbuf[:, rows + s0:rows + s0 + n, :]


def _mlp_kernel(x_ref, g_ref, w1_ref, w2_ref, gfin_ref, out_ref, hbuf, *, final):
    d_ff = w2_ref.shape[0]
    x = x_ref[...]
    d_model = x.shape[1]
    hbuf[...] = _rms(x, g_ref[...]).astype(_BF16)
    acc = x
    for c in range(0, d_ff, FF_BLOCK):
        f = _dot(hbuf[...], w1_ref[:, c:c + FF_BLOCK])
        f = jnp.square(jnp.maximum(f, 0.0)).astype(_BF16)
        acc = acc + _dot(f, w2_ref[c:c + FF_BLOCK, :d_model])
    if final:
        acc = _rms(acc, gfin_ref[...])
        batch = out_ref.shape[0]
        out_ref[...] = jnp.swapaxes(acc.reshape(acc.shape[0] // batch, batch, acc.shape[1]), 0, 1)
    else:
        out_ref[...] = acc


def _layer_resident(a, layer):
    zeros = (0,) * (a.ndim - 1)
    return pl.BlockSpec((None,) + a.shape[1:], lambda i: (layer,) + zeros, pipeline_mode=pl.Buffered(1))


def _layer_bytes(*arrays):
    return sum(a.size // a.shape[0] * a.dtype.itemsize for a in arrays)


def _scratch_bytes(shapes):
    return sum(functools.reduce(lambda a, b: a * b, s) * jnp.dtype(d).itemsize for s, d in shapes)


def _vmem_limit(weights, row_block_bytes, scratch_bytes, temp_bytes):
    want = _layer_bytes(*weights) + 2 * 2 * row_block_bytes + scratch_bytes + temp_bytes
    return min(want, V7X_VMEM_BYTES)


def _mixer(x, layer, weights, *, batch, ranges):
    d_model = x.shape[-1]
    n_rows = x.size // d_model
    rows = MIXER_STEPS * batch
    b_in, caw, cbw = weights[2], weights[3], weights[8]
    _, n_groups, ka, _ = caw.shape
    assert n_groups * V7X_LANES == d_model
    _, kb, d_rnn = cbw.shape
    halo_a, halo_b = (ka - 1) * batch, (kb - 1) * batch
    n_gate = d_rnn // V7X_LANES
    scratch = (
        ((rows, d_model), _BF16),
        ((rows, n_groups * V7X_LANES), _BF16),
        ((rows, d_rnn), _BF16),
        ((n_groups, halo_a + rows, V7X_LANES), _BF16),
        ((n_groups, rows, V7X_LANES), _F32),
        ((rows, b_in.shape[2]), _F32),
        ((d_rnn // V7X_MXU_DIM, halo_b + rows, V7X_MXU_DIM), _F32),
        ((n_gate, rows, 2 * V7X_LANES), _F32),
        ((n_gate, rows, V7X_LANES), _F32),
        ((n_gate, rows, V7X_LANES), _F32),
        ((n_gate, rows, V7X_LANES), _F32),
        ((rows, d_model), _F32),
        ((n_gate, batch, V7X_LANES), _F32),
    )
    limit = _vmem_limit(weights, rows * d_model * 4, _scratch_bytes(scratch), 6 * rows * d_rnn * 4)
    row_spec = pl.BlockSpec((rows, d_model), lambda i: (i, 0))
    x_spec = row_spec if x.ndim == 2 else pl.BlockSpec((batch, MIXER_STEPS, d_model), lambda i: (0, i, 0))
    return pl.pallas_call(
        functools.partial(_mixer_kernel, batch=batch, ranges=ranges),
        grid=(n_rows // rows,),
        in_specs=[x_spec] + [_layer_resident(w, layer) for w in weights],
        out_specs=row_spec,
        out_shape=jax.ShapeDtypeStruct((n_rows, d_model), x.dtype),
        scratch_shapes=[pltpu.VMEM(s, d) for s, d in scratch],
        compiler_params=pltpu.CompilerParams(dimension_semantics=("arbitrary",), vmem_limit_bytes=limit),
        name="mixer",
    )(x, *weights)


def _mlp(x, layer, weights, *, batch, final):
    n_rows, d_model = x.shape
    rows = MLP_ROWS
    scratch = (((rows, d_model), _BF16),)
    limit = _vmem_limit(weights, rows * d_model * 4, _scratch_bytes(scratch), 4 * rows * FF_BLOCK * 4)
    row_spec = pl.BlockSpec((rows, d_model), lambda i: (i, 0))
    return pl.pallas_call(
        functools.partial(_mlp_kernel, final=final),
        grid=(n_rows // rows,),
        in_specs=[row_spec] + [_layer_resident(w, layer) for w in weights],
        out_specs=pl.BlockSpec((batch, rows // batch, d_model), lambda i: (0, i, 0)) if final else row_spec,
        out_shape=jax.ShapeDtypeStruct((batch, n_rows // batch, d_model) if final else x.shape, x.dtype),
        scratch_shapes=[pltpu.VMEM(s, d) for s, d in scratch],
        compiler_params=pltpu.CompilerParams(dimension_semantics=("arbitrary",), vmem_limit_bytes=limit),
        name="mlp",
    )(x, *weights)


def kernel(x, g_mix, w_in, b_in, conv_a_w, conv_a_b, ln_g, ln_b, w_a_out, conv_b_w, conv_b_b, w_rg_a, b_rg_a,
           w_rg_x, b_rg_x, lam, w_b_out, w_o, g_mlp, w_1, w_2, g_final):
    batch, seq, d_model = x.shape
    depth = w_in.shape[0]
    d_conv = conv_a_w.shape[2]
    n_heads, head_dim = w_rg_a.shape[1], w_rg_a.shape[2]
    assert seq % MIXER_STEPS == 0 and (batch * seq) % MLP_ROWS == 0
    assert (n_heads * head_dim) % V7X_MXU_DIM == 0 and batch % 16 == 0
    assert d_conv % V7X_MXU_DIM == 0
    ranges = _band_ranges(n_heads, head_dim)

    def rows_of(a):
        return a[:, None, :]

    def matmul_weight(w):
        if (w.shape[2] // V7X_LANES) % 8 == 0:
            w = jnp.pad(w, ((0, 0), (0, 0), (0, V7X_LANES)))
        return w.astype(_BF16)

    mixer_weights = (
        rows_of(g_mix),
        matmul_weight(w_in),
        rows_of(b_in),
        _col_blocks(conv_a_w.astype(_BF16), V7X_LANES),
        _col_blocks(rows_of(conv_a_b), V7X_LANES),
        rows_of(ln_g),
        rows_of(ln_b),
        matmul_weight(w_a_out),
        conv_b_w,
        rows_of(conv_b_b),
        _pack_gates(w_rg_a, w_rg_x, ranges).astype(_BF16),
        _col_blocks(jnp.stack([b_rg_a, b_rg_x], axis=1), V7X_LANES),
        _col_blocks(rows_of(lam), V7X_LANES),
        matmul_weight(w_b_out),
        matmul_weight(w_o),
    )
    mlp_weights = (rows_of(g_mlp), w_1, w_2,
                   jnp.broadcast_to(g_final[None, None, :], (depth, 1, d_model)))

    for l in range(depth):
        x = _mixer(x, l, mixer_weights, batch=batch, ranges=ranges)
        x = _mlp(x, l, mlp_weights, batch=batch, final=(l == depth - 1))
    return x
```

```python
import functools

import jax
import jax.numpy as jnp
from jax import lax
from jax.experimental import pallas as pl
from jax.experimental.pallas import tpu as pltpu

LRU_C = 8.0
EPS = 1e-6

V7X_LANES = 128
V7X_MXU_DIM = 256
V7X_VMEM_BYTES = 64 * 1024 * 1024

MIXER_STEPS = 16
MLP_ROWS = 512
FF_BLOCK = 1024
CONV_GROUP = 16

_BF16 = jnp.bfloat16
_F32 = jnp.float32


def _dot(a, b):
    return lax.dot_general(a, b, (((1,), (0,)), ((), ())), preferred_element_type=_F32)


def _rms(x, g):
    return x * lax.rsqrt(jnp.mean(x * x, axis=-1, keepdims=True) + EPS) * g


def _sigmoid(x):
    return 0.5 * jnp.tanh(0.5 * x) + 0.5


def _gelu_tanh(x):
    c0 = 0.7978845608028654
    z = x * (x * x * (c0 * 0.044715) + c0)
    half = 0.5 * x
    return half * jnp.tanh(z) + half


def _band_ranges(n_heads, head_dim):
    d = n_heads * head_dim
    out = []
    for c0 in range(0, d, V7X_MXU_DIM):
        c1 = c0 + V7X_MXU_DIM
        k0 = (c0 // head_dim) * head_dim
        k1 = ((c1 - 1) // head_dim + 1) * head_dim
        k0 = k0 // V7X_LANES * V7X_LANES
        k1 = min(-(-k1 // V7X_LANES) * V7X_LANES, d)
        out.append((k0, k1))
    return tuple(out)


def _pack_gates(w_a, w_x, ranges):
    hd = w_a.shape[2]
    parts = []
    for j, (k0, k1) in enumerate(ranges):
        c0, c1 = j * V7X_MXU_DIM, (j + 1) * V7X_MXU_DIM
        halves = []
        for w in (w_a, w_x):
            tile = 0.0
            for h in range(c0 // hd, (c1 - 1) // hd + 1):
                lo, hi = max(h * hd, c0), min((h + 1) * hd, c1)
                piece = w[:, h, :, lo - h * hd:hi - h * hd]
                tile = tile + jnp.pad(piece, ((0, 0), (h * hd - k0, k1 - (h + 1) * hd), (lo - c0, c1 - hi)))
            halves.append(tile)
        parts.append(jnp.concatenate(halves, axis=2))
    return jnp.concatenate(parts, axis=1)


def _col_blocks(w, n):
    l, k, _ = w.shape
    return jnp.transpose(w.reshape(l, k, -1, n), (0, 2, 1, 3))


def _mixer_kernel(x_ref, g_ref, win_ref, bin_ref, caw_ref, cab_ref, lng_ref, lnb_ref,
                  waout_ref, cbw_ref, cbb_ref, wgate_ref, brg_ref, lam_ref, wbout_ref, wo_ref,
                  out_ref,
                  hbuf, actbuf, lhsbuf, ubuf, cbuf, pbuf, zbuf, gbuf, vbuf, abuf, bbuf, mbuf, hstate,
                  *, batch, ranges):
    rows, d_model = out_ref.shape
    n_groups, ka, _ = caw_ref.shape
    kb, d_rnn = cbw_ref.shape
    d_in = bin_ref.shape[1]
    d_conv = n_groups * V7X_LANES
    halo_a = (ka - 1) * batch
    halo_b = (kb - 1) * batch
    o_ga, o_xb = d_conv, 2 * d_conv
    o_gb = o_xb + d_rnn
    o_sa = o_gb + d_rnn
    o_sb = o_sa + d_model
    n_xb = d_rnn // V7X_MXU_DIM
    pid = pl.program_id(0)

    @pl.when(pid == 0)
    def _():
        ubuf[:, 0:halo_a, :] = jnp.zeros((n_groups, halo_a, V7X_LANES), _BF16)
        zbuf[:, 0:halo_b, :] = jnp.zeros((n_xb, halo_b, V7X_MXU_DIM), _F32)
        hstate[...] = jnp.zeros(hstate.shape, _F32)

    def conv_group(g):
        acc = jnp.broadcast_to(cab_ref[g], (rows, V7X_LANES))
        for k0 in range(0, ka, CONV_GROUP):
            part = None
            for k in range(k0, min(k0 + CONV_GROUP, ka)):
                tap = caw_ref[g, k:k + 1, :] * ubuf[g, k * batch:k * batch + rows, :]
                part = tap if part is None else part + tap
            acc = acc + part.astype(_F32)
        cbuf[g] = acc

    if len(x_ref.shape) == 3:
        x_rows = jnp.swapaxes(x_ref[...], 0, 1).reshape(rows, d_model)
    else:
        x_rows = x_ref[...]
    hbuf[...] = _rms(x_rows, g_ref[...]).astype(_BF16)
    pbuf[...] = _dot(hbuf[...], win_ref[:, :d_in]) + bin_ref[...]

    for g in range(n_groups):
        c = g * V7X_LANES
        glu = pbuf[:, c:c + V7X_LANES] * _sigmoid(pbuf[:, o_ga + c:o_ga + c + V7X_LANES])
        ubuf[g, halo_a:halo_a + rows, :] = glu.astype(_BF16)

    for i in range(n_xb):
        c = i * V7X_MXU_DIM
        zbuf[i, halo_b:halo_b + rows, :] = pbuf[:, o_xb + c:o_xb + c + V7X_MXU_DIM]
        v = jnp.broadcast_to(cbb_ref[:, c:c + V7X_MXU_DIM], (rows, V7X_MXU_DIM))
        for k in range(kb):
            v = v + cbw_ref[k:k + 1, c:c + V7X_MXU_DIM] * zbuf[i, k * batch:k * batch + rows, :]
        for half in range(V7X_MXU_DIM // V7X_LANES):
            vbuf[2 * i + half] = v[:, half * V7X_LANES:(half + 1) * V7X_LANES]
        lhsbuf[:, c:c + V7X_MXU_DIM] = v.astype(_BF16)
    for c in range(o_gb, o_sa, V7X_MXU_DIM):
        pbuf[:, c:c + V7X_MXU_DIM] = _gelu_tanh(pbuf[:, c:c + V7X_MXU_DIM])
    for c in range(o_sa, d_in, V7X_MXU_DIM):
        pbuf[:, c:c + V7X_MXU_DIM] = _sigmoid(pbuf[:, c:c + V7X_MXU_DIM])

    for g in range(n_groups):
        conv_group(g)
    cv = jnp.concatenate([cbuf[g] for g in range(n_groups)], axis=1)
    mu = jnp.mean(cv, axis=-1, keepdims=True)
    xc = cv - mu
    yn = xc * lax.rsqrt(jnp.mean(xc * xc, axis=-1, keepdims=True) + EPS) * lng_ref[...] + lnb_ref[...]
    actbuf[...] = (yn * _sigmoid(yn)).astype(_BF16)

    off = 0
    for j, (k0, k1) in enumerate(ranges):
        z = _dot(lhsbuf[:, k0:k1], wgate_ref[off:off + (k1 - k0), :])
        off += k1 - k0
        for half in range(V7X_MXU_DIM // V7X_LANES):
            c = half * V7X_LANES
            gbuf[2 * j + half] = jnp.concatenate(
                [z[:, c:c + V7X_LANES], z[:, V7X_MXU_DIM + c:V7X_MXU_DIM + c + V7X_LANES]], axis=1)
    y_a = _dot(actbuf[...], waout_ref[:, :d_model])
    mbuf[...] = pbuf[:, o_sa:o_sa + d_model] * y_a

    row = pid * rows + lax.broadcasted_iota(jnp.int32, (rows, 1), 0)
    is_start = row < batch

    def gate_body(q, carry):
        gates = gbuf[q]
        bias = brg_ref[q]
        x_lam = -lam_ref[q]
        softplus_neg_lam = jnp.maximum(x_lam, 0.0) + jnp.log1p(jnp.exp(-jnp.abs(x_lam)))
        i_gate = _sigmoid(gates[:, V7X_LANES:] + bias[1:2, :])
        tanh_r = jnp.tanh(0.5 * (gates[:, :V7X_LANES] + bias[0:1, :]))
        log_a = (tanh_r + 1.0) * ((-0.5 * LRU_C) * softplus_neg_lam)
        a = jnp.exp(log_a)
        s = jnp.tanh(-log_a) * (1.0 + a * a)
        mult = jnp.where(s > 0.0, s * lax.rsqrt(s), 0.0)
        mult = jnp.where(is_start, 1.0, mult)
        abuf[q] = a
        bbuf[q] = mult * i_gate * vbuf[q]
        return carry

    for q in range(gbuf.shape[0]):
        gate_body(q, 0)

    h = hstate[...]
    for t in range(rows // batch):
        r0 = t * batch
        h = abuf[:, r0:r0 + batch, :] * h + bbuf[:, r0:r0 + batch, :]
        bbuf[:, r0:r0 + batch, :] = h
    hstate[...] = h

    for i in range(n_xb):
        c = i * V7X_MXU_DIM
        state = jnp.concatenate([bbuf[2 * i], bbuf[2 * i + 1]], axis=1)
        lhsbuf[:, c:c + V7X_MXU_DIM] = (state * pbuf[:, o_gb + c:o_gb + c + V7X_MXU_DIM]).astype(_BF16)
    y_b = _dot(lhsbuf[...], wbout_ref[:, :d_model])
    actbuf[...] = (mbuf[...] + pbuf[:, o_sb:o_sb + d_model] * y_b).astype(_BF16)
    out_ref[...] = x_rows + _dot(actbuf[...], wo_ref[:, :d_model])

    for s0 in range(0, halo_a, rows):
        n = min(rows, halo_a - s0)
        ubuf[:, s0:s0 + n, :] = ubuf[:, rows + s0:rows + s0 + n, :]
    for s0 in range(0, halo_b, rows):
        n = min(rows, halo_b - s0)
        zbuf[:, s0:s0 + n, :] = zbuf[:, rows + s0:rows + s0 + n, :]


def _mlp_kernel(x_ref, g_ref, w1_ref, w2_ref, gfin_ref, out_ref, hbuf, *, final):
    d_ff = w2_ref.shape[0]
    x = x_ref[...]
    d_model = x.shape[1]
    hbuf[...] = _rms(x, g_ref[...]).astype(_BF16)
    acc = x
    for c in range(0, d_ff, FF_BLOCK):
        f = _dot(hbuf[...], w1_ref[:, c:c + FF_BLOCK])
        f = jnp.square(jnp.maximum(f, 0.0)).astype(_BF16)
        acc = acc + _dot(f, w2_ref[c:c + FF_BLOCK, :d_model])
    if final:
        acc = _rms(acc, gfin_ref[...])
        batch = out_ref.shape[0]
        out_ref[...] = jnp.swapaxes(acc.reshape(acc.shape[0] // batch, batch, acc.shape[1]), 0, 1)
    else:
        out_ref[...] = acc


def _layer_resident(a, layer):
    zeros = (0,) * (a.ndim - 1)
    return pl.BlockSpec((None,) + a.shape[1:], lambda i: (layer,) + zeros, pipeline_mode=pl.Buffered(1))


def _layer_bytes(*arrays):
    return sum(a.size // a.shape[0] * a.dtype.itemsize for a in arrays)


def _scratch_bytes(shapes):
    return sum(functools.reduce(lambda a, b: a * b, s) * jnp.dtype(d).itemsize for s, d in shapes)


def _vmem_limit(weights, row_block_bytes, scratch_bytes, temp_bytes):
    want = _layer_bytes(*weights) + 2 * 2 * row_block_bytes + scratch_bytes + temp_bytes
    return min(want, V7X_VMEM_BYTES)


def _mixer(x, layer, weights, *, batch, ranges):
    d_model = x.shape[-1]
    n_rows = x.size // d_model
    rows = MIXER_STEPS * batch
    b_in, caw, cbw = weights[2], weights[3], weights[8]
    _, n_groups, ka, _ = caw.shape
    assert n_groups * V7X_LANES == d_model
    _, kb, d_rnn = cbw.shape
    halo_a, halo_b = (ka - 1) * batch, (kb - 1) * batch
    n_gate = d_rnn // V7X_LANES
    scratch = (
        ((rows, d_model), _BF16),
        ((rows, n_groups * V7X_LANES), _BF16),
        ((rows, d_rnn), _BF16),
        ((n_groups, halo_a + rows, V7X_LANES), _BF16),
        ((n_groups, rows, V7X_LANES), _F32),
        ((rows, b_in.shape[2]), _F32),
        ((d_rnn // V7X_MXU_DIM, halo_b + rows, V7X_MXU_DIM), _F32),
        ((n_gate, rows, 2 * V7X_LANES), _F32),
        ((n_gate, rows, V7X_LANES), _F32),
        ((n_gate, rows, V7X_LANES), _F32),
        ((n_gate, rows, V7X_LANES), _F32),
        ((rows, d_model), _F32),
        ((n_gate, batch, V7X_LANES), _F32),
    )
    limit = _vmem_limit(weights, rows * d_model * 4, _scratch_bytes(scratch), 6 * rows * d_rnn * 4)
    row_spec = pl.BlockSpec((rows, d_model), lambda i: (i, 0))
    x_spec = row_spec if x.ndim == 2 else pl.BlockSpec((batch, MIXER_STEPS, d_model), lambda i: (0, i, 0))
    return pl.pallas_call(
        functools.partial(_mixer_kernel, batch=batch, ranges=ranges),
        grid=(n_rows // rows,),
        in_specs=[x_spec] + [_layer_resident(w, layer) for w in weights],
        out_specs=row_spec,
        out_shape=jax.ShapeDtypeStruct((n_rows, d_model), x.dtype),
        scratch_shapes=[pltpu.VMEM(s, d) for s, d in scratch],
        compiler_params=pltpu.CompilerParams(dimension_semantics=("arbitrary",), vmem_limit_bytes=limit),
        name="mixer",
    )(x, *weights)


def _mlp(x, layer, weights, *, batch, final):
    n_rows, d_model = x.shape
    rows = MLP_ROWS
    scratch = (((rows, d_model), _BF16),)
    limit = _vmem_limit(weights, rows * d_model * 4, _scratch_bytes(scratch), 4 * rows * FF_BLOCK * 4)
    row_spec = pl.BlockSpec((rows, d_model), lambda i: (i, 0))
    return pl.pallas_call(
        functools.partial(_mlp_kernel, final=final),
        grid=(n_rows // rows,),
        in_specs=[row_spec] + [_layer_resident(w, layer) for w in weights],
        out_specs=pl.BlockSpec((batch, rows // batch, d_model), lambda i: (0, i, 0)) if final else row_spec,
        out_shape=jax.ShapeDtypeStruct((batch, n_rows // batch, d_model) if final else x.shape, x.dtype),
        scratch_shapes=[pltpu.VMEM(s, d) for s, d in scratch],
        compiler_params=pltpu.CompilerParams(dimension_semantics=("arbitrary",), vmem_limit_bytes=limit),
        name="mlp",
    )(x, *weights)


def kernel(x, g_mix, w_in, b_in, conv_a_w, conv_a_b, ln_g, ln_b, w_a_out, conv_b_w, conv_b_b, w_rg_a, b_rg_a,
           w_rg_x, b_rg_x, lam, w_b_out, w_o, g_mlp, w_1, w_2, g_final):
    batch, seq, d_model = x.shape
    depth = w_in.shape[0]
    d_conv = conv_a_w.shape[2]
    n_heads, head_dim = w_rg_a.shape[1], w_rg_a.shape[2]
    assert seq % MIXER_STEPS == 0 and (batch * seq) % MLP_ROWS == 0
    assert (n_heads * head_dim) % V7X_MXU_DIM == 0 and batch % 16 == 0
    assert d_conv % V7X_MXU_DIM == 0
    ranges = _band_ranges(n_heads, head_dim)

    def rows_of(a):
        return a[:, None, :]

    def matmul_weight(w):
        if (w.shape[2] // V7X_LANES) % 8 == 0:
            w = jnp.pad(w, ((0, 0), (0, 0), (0, V7X_LANES)))
        return w.astype(_BF16)

    mixer_weights = (
        rows_of(g_mix),
        matmul_weight(w_in),
        rows_of(b_in),
        _col_blocks(conv_a_w.astype(_BF16), V7X_LANES),
        _col_blocks(rows_of(conv_a_b), V7X_LANES),
        rows_of(ln_g),
        rows_of(ln_b),
        matmul_weight(w_a_out),
        conv_b_w,
        rows_of(conv_b_b),
        _pack_gates(w_rg_a, w_rg_x, ranges).astype(_BF16),
        _col_blocks(jnp.stack([b_rg_a, b_rg_x], axis=1), V7X_LANES),
        _col_blocks(rows_of(lam), V7X_LANES),
        matmul_weight(w_b_out),
        matmul_weight(w_o),
    )
    mlp_weights = (rows_of(g_mlp), w_1, w_2,
                   jnp.broadcast_to(g_final[None, None, :], (depth, 1, d_model)))

    for l in range(depth):
        x = _mixer(x, l, mixer_weights, batch=batch, ranges=ranges)
        x = _mlp(x, l, mlp_weights, batch=batch, final=(l == depth - 1))
    return x
```

```python
import functools

import jax
import jax.numpy as jnp
from jax import lax
from jax.experimental import pallas as pl
from jax.experimental.pallas import tpu as pltpu

LRU_C = 8.0
EPS = 1e-6

V7X_LANES = 128
V7X_MXU_DIM = 256
V7X_VMEM_BYTES = 64 * 1024 * 1024

MIXER_STEPS = 16
MLP_ROWS = 512
FF_BLOCK = 1024
CONV_GROUP = 16

_BF16 = jnp.bfloat16
_F32 = jnp.float32


def _dot(a, b):
    return lax.dot_general(a, b, (((1,), (0,)), ((), ())), preferred_element_type=_F32)


def _rms(x, g):
    return x * lax.rsqrt(jnp.mean(x * x, axis=-1, keepdims=True) + EPS) * g


def _sigmoid(x):
    return 0.5 * jnp.tanh(0.5 * x) + 0.5


def _gelu_tanh(x):
    c0 = 0.7978845608028654
    z = x * (x * x * (c0 * 0.044715) + c0)
    half = 0.5 * x
    return half * jnp.tanh(z) + half


def _band_ranges(n_heads, head_dim):
    d = n_heads * head_dim
    out = []
    for c0 in range(0, d, V7X_MXU_DIM):
        c1 = c0 + V7X_MXU_DIM
        k0 = (c0 // head_dim) * head_dim
        k1 = ((c1 - 1) // head_dim + 1) * head_dim
        k0 = k0 // V7X_LANES * V7X_LANES
        k1 = min(-(-k1 // V7X_LANES) * V7X_LANES, d)
        out.append((k0, k1))
    return tuple(out)


def _pack_gates(w_a, w_x, ranges):
    hd = w_a.shape[2]
    parts = []
    for j, (k0, k1) in enumerate(ranges):
        c0, c1 = j * V7X_MXU_DIM, (j + 1) * V7X_MXU_DIM
        halves = []
        for w in (w_a, w_x):
            tile = 0.0
            for h in range(c0 // hd, (c1 - 1) // hd + 1):
                lo, hi = max(h * hd, c0), min((h + 1) * hd, c1)
                piece = w[:, h, :, lo - h * hd:hi - h * hd]
                tile = tile + jnp.pad(piece, ((0, 0), (h * hd - k0, k1 - (h + 1) * hd), (lo - c0, c1 - hi)))
            halves.append(tile)
        parts.append(jnp.concatenate(halves, axis=2))
    return jnp.concatenate(parts, axis=1)


def _col_blocks(w, n):
    l, k, _ = w.shape
    return jnp.transpose(w.reshape(l, k, -1, n), (0, 2, 1, 3))


def _mixer_kernel(x_ref, g_ref, win_ref, bin_ref, caw_ref, cab_ref, lng_ref, lnb_ref,
                  waout_ref, cbw_ref, cbb_ref, wgate_ref, brg_ref, lam_ref, wbout_ref, wo_ref,
                  out_ref,
                  hbuf, actbuf, lhsbuf, ubuf, cbuf, pbuf, zbuf, gbuf, vbuf, abuf, bbuf, mbuf, hstate,
                  *, batch, ranges):
    rows, d_model = out_ref.shape
    n_groups, ka, _ = caw_ref.shape
    kb, d_rnn = cbw_ref.shape
    d_in = bin_ref.shape[1]
    d_conv = n_groups * V7X_LANES
    halo_a = (ka - 1) * batch
    halo_b = (kb - 1) * batch
    o_ga, o_xb = d_conv, 2 * d_conv
    o_gb = o_xb + d_rnn
    o_sa = o_gb + d_rnn
    o_sb = o_sa + d_model
    n_xb = d_rnn // V7X_MXU_DIM
    pid = pl.program_id(0)

    @pl.when(pid == 0)
    def _():
        ubuf[:, 0:halo_a, :] = jnp.zeros((n_groups, halo_a, V7X_LANES), _BF16)
        zbuf[:, 0:halo_b, :] = jnp.zeros((n_xb, halo_b, V7X_MXU_DIM), _F32)
        hstate[...] = jnp.zeros(hstate.shape, _F32)

    def conv_group(g):
        acc = jnp.broadcast_to(cab_ref[g], (rows, V7X_LANES))
        for k0 in range(0, ka, CONV_GROUP):
            taps = [caw_ref[g, k:k + 1, :] * ubuf[g, k * batch:k * batch + rows, :]
                    for k in range(k0, min(k0 + CONV_GROUP, ka))]
            while len(taps) > 1:
                taps = [taps[i] + taps[i + 1] for i in range(0, len(taps) - 1, 2)] + taps[len(taps) - len(taps) % 2:]
            acc = acc + taps[0].astype(_F32)
        cbuf[g] = acc

    if len(x_ref.shape) == 3:
        x_rows = jnp.swapaxes(x_ref[...], 0, 1).reshape(rows, d_model)
    else:
        x_rows = x_ref[...]
    hbuf[...] = _rms(x_rows, g_ref[...]).astype(_BF16)
    pbuf[...] = _dot(hbuf[...], win_ref[:, :d_in]) + bin_ref[...]

    for g in range(n_groups):
        c = g * V7X_LANES
        glu = pbuf[:, c:c + V7X_LANES] * _sigmoid(pbuf[:, o_ga + c:o_ga + c + V7X_LANES])
        ubuf[g, halo_a:halo_a + rows, :] = glu.astype(_BF16)

    for i in range(n_xb):
        c = i * V7X_MXU_DIM
        zbuf[i, halo_b:halo_b + rows, :] = pbuf[:, o_xb + c:o_xb + c + V7X_MXU_DIM]
        v = jnp.broadcast_to(cbb_ref[:, c:c + V7X_MXU_DIM], (rows, V7X_MXU_DIM))
        for k in range(kb):
            v = v + cbw_ref[k:k + 1, c:c + V7X_MXU_DIM] * zbuf[i, k * batch:k * batch + rows, :]
        for half in range(V7X_MXU_DIM // V7X_LANES):
            vbuf[2 * i + half] = v[:, half * V7X_LANES:(half + 1) * V7X_LANES]
        lhsbuf[:, c:c + V7X_MXU_DIM] = v.astype(_BF16)
    for c in range(o_gb, o_sa, V7X_MXU_DIM):
        pbuf[:, c:c + V7X_MXU_DIM] = _gelu_tanh(pbuf[:, c:c + V7X_MXU_DIM])
    for c in range(o_sa, d_in, V7X_MXU_DIM):
        pbuf[:, c:c + V7X_MXU_DIM] = _sigmoid(pbuf[:, c:c + V7X_MXU_DIM])

    for g in range(n_groups):
        conv_group(g)
    cv = jnp.concatenate([cbuf[g] for g in range(n_groups)], axis=1)
    mu = jnp.mean(cv, axis=-1, keepdims=True)
    xc = cv - mu
    yn = xc * lax.rsqrt(jnp.mean(xc * xc, axis=-1, keepdims=True) + EPS) * lng_ref[...] + lnb_ref[...]
    actbuf[...] = (yn * _sigmoid(yn)).astype(_BF16)

    off = 0
    for j, (k0, k1) in enumerate(ranges):
        z = _dot(lhsbuf[:, k0:k1], wgate_ref[off:off + (k1 - k0), :])
        off += k1 - k0
        for half in range(V7X_MXU_DIM // V7X_LANES):
            c = half * V7X_LANES
            gbuf[2 * j + half] = jnp.concatenate(
                [z[:, c:c + V7X_LANES], z[:, V7X_MXU_DIM + c:V7X_MXU_DIM + c + V7X_LANES]], axis=1)
    y_a = _dot(actbuf[...], waout_ref[:, :d_model])
    mbuf[...] = pbuf[:, o_sa:o_sa + d_model] * y_a

    row = pid * rows + lax.broadcasted_iota(jnp.int32, (rows, 1), 0)
    is_start = row < batch

    def gate_body(q, carry):
        gates = gbuf[q]
        bias = brg_ref[q]
        x_lam = -lam_ref[q]
        softplus_neg_lam = jnp.maximum(x_lam, 0.0) + jnp.log1p(jnp.exp(-jnp.abs(x_lam)))
        i_gate = _sigmoid(gates[:, V7X_LANES:] + bias[1:2, :])
        tanh_r = jnp.tanh(0.5 * (gates[:, :V7X_LANES] + bias[0:1, :]))
        log_a = (tanh_r + 1.0) * ((-0.5 * LRU_C) * softplus_neg_lam)
        a = jnp.exp(log_a)
        s = jnp.tanh(-log_a) * (1.0 + a * a)
        mult = jnp.where(s > 0.0, s * lax.rsqrt(s), 0.0)
        mult = jnp.where(is_start, 1.0, mult)
        abuf[q] = a
        bbuf[q] = mult * i_gate * vbuf[q]
        return carry

    for q in range(gbuf.shape[0]):
        gate_body(q, 0)

    h = hstate[...]
    for t in range(rows // batch):
        r0 = t * batch
        h = abuf[:, r0:r0 + batch, :] * h + bbuf[:, r0:r0 + batch, :]
        bbuf[:, r0:r0 + batch, :] = h
    hstate[...] = h

    for i in range(n_xb):
        c = i * V7X_MXU_DIM
        state = jnp.concatenate([bbuf[2 * i], bbuf[2 * i + 1]], axis=1)
        lhsbuf[:, c:c + V7X_MXU_DIM] = (state * pbuf[:, o_gb + c:o_gb + c + V7X_MXU_DIM]).astype(_BF16)
    y_b = _dot(lhsbuf[...], wbout_ref[:, :d_model])
    actbuf[...] = (mbuf[...] + pbuf[:, o_sb:o_sb + d_model] * y_b).astype(_BF16)
    out_ref[...] = x_rows + _dot(actbuf[...], wo_ref[:, :d_model])

    for s0 in range(0, halo_a, rows):
        n = min(rows, halo_a - s0)
        ubuf[:, s0:s0 + n, :] = ubuf[:, rows + s0:rows + s0 + n, :]
    for s0 in range(0, halo_b, rows):
        n = min(rows, halo_b - s0)
        zbuf[:, s0:s0 + n, :] = zbuf[:, rows + s0:rows + s0 + n, :]


def _mlp_kernel(x_ref, g_ref, w1_ref, w2_ref, gfin_ref, out_ref, hbuf, *, final):
    d_ff = w2_ref.shape[0]
    x = x_ref[...]
    d_model = x.shape[1]
    hbuf[...] = _rms(x, g_ref[...]).astype(_BF16)
    acc = x
    for c in range(0, d_ff, FF_BLOCK):
        f = _dot(hbuf[...], w1_ref[:, c:c + FF_BLOCK])
        f = jnp.square(jnp.maximum(f, 0.0)).astype(_BF16)
        acc = acc + _dot(f, w2_ref[c:c + FF_BLOCK, :d_model])
    if final:
        acc = _rms(acc, gfin_ref[...])
        batch = out_ref.shape[0]
        out_ref[...] = jnp.swapaxes(acc.reshape(acc.shape[0] // batch, batch, acc.shape[1]), 0, 1)
    else:
        out_ref[...] = acc


def _layer_resident(a, layer):
    zeros = (0,) * (a.ndim - 1)
    return pl.BlockSpec((None,) + a.shape[1:], lambda i: (layer,) + zeros, pipeline_mode=pl.Buffered(1))


def _layer_bytes(*arrays):
    return sum(a.size // a.shape[0] * a.dtype.itemsize for a in arrays)


def _scratch_bytes(shapes):
    return sum(functools.reduce(lambda a, b: a * b, s) * jnp.dtype(d).itemsize for s, d in shapes)


def _vmem_limit(weights, row_block_bytes, scratch_bytes, temp_bytes):
    want = _layer_bytes(*weights) + 2 * 2 * row_block_bytes + scratch_bytes + temp_bytes
    return min(want, V7X_VMEM_BYTES)


def _mixer(x, layer, weights, *, batch, ranges):
    d_model = x.shape[-1]
    n_rows = x.size // d_model
    rows = MIXER_STEPS * batch
    b_in, caw, cbw = weights[2], weights[3], weights[8]
    _, n_groups, ka, _ = caw.shape
    assert n_groups * V7X_LANES == d_model
    _, kb, d_rnn = cbw.shape
    halo_a, halo_b = (ka - 1) * batch, (kb - 1) * batch
    n_gate = d_rnn // V7X_LANES
    scratch = (
        ((rows, d_model), _BF16),
        ((rows, n_groups * V7X_LANES), _BF16),
        ((rows, d_rnn), _BF16),
        ((n_groups, halo_a + rows, V7X_LANES), _BF16),
        ((n_groups, rows, V7X_LANES), _F32),
        ((rows, b_in.shape[2]), _F32),
        ((d_rnn // V7X_MXU_DIM, halo_b + rows, V7X_MXU_DIM), _F32),
        ((n_gate, rows, 2 * V7X_LANES), _F32),
        ((n_gate, rows, V7X_LANES), _F32),
        ((n_gate, rows, V7X_LANES), _F32),
        ((n_gate, rows, V7X_LANES), _F32),
        ((rows, d_model), _F32),
        ((n_gate, batch, V7X_LANES), _F32),
    )
    limit = _vmem_limit(weights, rows * d_model * 4, _scratch_bytes(scratch), 6 * rows * d_rnn * 4)
    row_spec = pl.BlockSpec((rows, d_model), lambda i: (i, 0))
    x_spec = row_spec if x.ndim == 2 else pl.BlockSpec((batch, MIXER_STEPS, d_model), lambda i: (0, i, 0))
    return pl.pallas_call(
        functools.partial(_mixer_kernel, batch=batch, ranges=ranges),
        grid=(n_rows // rows,),
        in_specs=[x_spec] + [_layer_resident(w, layer) for w in weights],
        out_specs=row_spec,
        out_shape=jax.ShapeDtypeStruct((n_rows, d_model), x.dtype),
        scratch_shapes=[pltpu.VMEM(s, d) for s, d in scratch],
        compiler_params=pltpu.CompilerParams(dimension_semantics=("arbitrary",), vmem_limit_bytes=limit),
        name="mixer",
    )(x, *weights)


def _mlp(x, layer, weights, *, batch, final):
    n_rows, d_model = x.shape
    rows = MLP_ROWS
    scratch = (((rows, d_model), _BF16),)
    limit = _vmem_limit(weights, rows * d_model * 4, _scratch_bytes(scratch), 4 * rows * FF_BLOCK * 4)
    row_spec = pl.BlockSpec((rows, d_model), lambda i: (i, 0))
    return pl.pallas_call(
        functools.partial(_mlp_kernel, final=final),
        grid=(n_rows // rows,),
        in_specs=[row_spec] + [_layer_resident(w, layer) for w in weights],
        out_specs=pl.BlockSpec((batch, rows // batch, d_model), lambda i: (0, i, 0)) if final else row_spec,
        out_shape=jax.ShapeDtypeStruct((batch, n_rows // batch, d_model) if final else x.shape, x.dtype),
        scratch_shapes=[pltpu.VMEM(s, d) for s, d in scratch],
        compiler_params=pltpu.CompilerParams(dimension_semantics=("arbitrary",), vmem_limit_bytes=limit),
        name="mlp",
    )(x, *weights)


def kernel(x, g_mix, w_in, b_in, conv_a_w, conv_a_b, ln_g, ln_b, w_a_out, conv_b_w, conv_b_b, w_rg_a, b_rg_a,
           w_rg_x, b_rg_x, lam, w_b_out, w_o, g_mlp, w_1, w_2, g_final):
    batch, seq, d_model = x.shape
    depth = w_in.shape[0]
    d_conv = conv_a_w.shape[2]
    n_heads, head_dim = w_rg_a.shape[1], w_rg_a.shape[2]
    assert seq % MIXER_STEPS == 0 and (batch * seq) % MLP_ROWS == 0
    assert (n_heads * head_dim) % V7X_MXU_DIM == 0 and batch % 16 == 0
    assert d_conv % V7X_MXU_DIM == 0
    ranges = _band_ranges(n_heads, head_dim)

    def rows_of(a):
        return a[:, None, :]

    def matmul_weight(w):
        if (w.shape[2] // V7X_LANES) % 8 == 0:
            w = jnp.pad(w, ((0, 0), (0, 0), (0, V7X_LANES)))
        return w.astype(_BF16)

    mixer_weights = (
        rows_of(g_mix),
        matmul_weight(w_in),
        rows_of(b_in),
        _col_blocks(conv_a_w.astype(_BF16), V7X_LANES),
        _col_blocks(rows_of(conv_a_b), V7X_LANES),
        rows_of(ln_g),
        rows_of(ln_b),
        matmul_weight(w_a_out),
        conv_b_w,
        rows_of(conv_b_b),
        _pack_gates(w_rg_a, w_rg_x, ranges).astype(_BF16),
        _col_blocks(jnp.stack([b_rg_a, b_rg_x], axis=1), V7X_LANES),
        _col_blocks(rows_of(lam), V7X_LANES),
        matmul_weight(w_b_out),
        matmul_weight(w_o),
    )
    mlp_weights = (rows_of(g_mlp), w_1, w_2,
                   jnp.broadcast_to(g_final[None, None, :], (depth, 1, d_model)))

    for l in range(depth):
        x = _mixer(x, l, mixer_weights, batch=batch, ranges=ranges)
        x = _mlp(x, l, mlp_weights, batch=batch, final=(l == depth - 1))
    return x
```
